```python
import jax
import jax.numpy as jnp
from jax import lax
import numpy as np

D_MODEL = 2048
BATCH = 1
SEQ = 8192
DEPTH = 4

MIXER_PATTERN = ('short_conv', 'hgrn2', 'dsa')
N_MIXERS = 3
RMS_EPS = 1e-6
ROPE_THETA = 10000.0
RESID_SCALE = 0.35

CONV_WIDTH = 3

HG_EXPAND = 128
HG_HEADS = D_MODEL // HG_EXPAND
HG_DK = HG_EXPAND
HG_DV = D_MODEL // HG_HEADS
HG_CHUNK = 64

ATT_HEADS = 16
ATT_KV_HEADS = 4
ATT_HEAD_DIM = D_MODEL // ATT_HEADS
ATT_GROUP = ATT_HEADS // ATT_KV_HEADS
IDX_HEADS = 16
IDX_HEAD_DIM = 128
IDX_ROPE_DIM = 64
TOPK_MAX = 256
Q_BLOCK = 128
DSA_SPLITS = (ATT_HEADS * ATT_HEAD_DIM, ATT_KV_HEADS * ATT_HEAD_DIM, ATT_KV_HEADS * ATT_HEAD_DIM,
              IDX_HEADS * IDX_HEAD_DIM, IDX_HEAD_DIM, IDX_HEADS)
DSA_IN_DIM = sum(DSA_SPLITS)

N_EXPERTS = 32
MOE_TOP_K = 4
D_EXPERT = D_MODEL // 2
SWIGLU_LIMIT = 7.0
SWIGLU_ALPHA = 1.702

kernel_name = 'hybrid_conv_hgrn2_dsa_moe_trunk'


def rmsnorm(x, gain):
    xf = x.astype(jnp.float32)
    y = xf * lax.rsqrt(jnp.mean(xf * xf, axis=-1, keepdims=True) + RMS_EPS)
    return (y * gain.astype(jnp.float32)).astype(x.dtype)


def rope_cos_sin(n_pos, dim):
    inv_freq = 1.0 / (ROPE_THETA ** (jnp.arange(0, dim, 2, dtype=jnp.float32) / dim))
    ang = jnp.arange(n_pos, dtype=jnp.float32)[:, None] * inv_freq[None, :]
    return jnp.cos(ang), jnp.sin(ang)


def apply_rope(x, cos, sin):
    half = cos.shape[-1]
    xf = x.astype(jnp.float32)
    x1, x2, rest = xf[..., :half], xf[..., half:2 * half], xf[..., 2 * half:]
    c, s = cos[:, None, :], sin[:, None, :]
    return jnp.concatenate([x1 * c - x2 * s, x2 * c + x1 * s, rest], axis=-1).astype(x.dtype)


def short_conv_mixer(x, w_in, conv_w, w_out):
    b_gate, c_gate, h = jnp.split(x @ w_in, 3, axis=-1)
    u = c_gate * h
    conv = lax.conv_general_dilated(
        u, conv_w[:, None, :].astype(u.dtype), window_strides=(1,),
        padding=[(CONV_WIDTH - 1, 0)], dimension_numbers=('NWC', 'WIO', 'NWC'),
        feature_group_count=D_MODEL)
    return (b_gate * conv) @ w_out


def hgrn2_mixer(x, w_in, g_norm, w_out, lb):
    B, T, _ = x.shape
    f32 = jnp.float32
    n_chunks = T // HG_CHUNK
    q, f, i, g = jnp.split(x @ w_in, 4, axis=-1)
    q = jax.nn.silu(q.astype(f32))
    forget = lb + (1.0 - lb) * jax.nn.sigmoid(f.astype(f32))
    k = 1.0 - forget
    log_f = jnp.log(forget)
    v = i.astype(f32)

    def chunks(t, d):
        return t.reshape(B, n_chunks, HG_CHUNK, HG_HEADS, d).transpose(1, 0, 3, 2, 4)

    causal = jnp.tril(jnp.ones((HG_CHUNK, HG_CHUNK), dtype=bool))

    def step(S, inp):
        qc, kc, vc, gc = inp
        b = jnp.cumsum(gc, axis=2)
        o_inter = jnp.einsum('bhtd,bhde->bhte', qc * jnp.exp(b), S)
        diff = b[:, :, :, None, :] - b[:, :, None, :, :]
        decay = jnp.exp(jnp.where(causal[:, :, None], diff, -jnp.inf))
        scores = jnp.einsum('bhtd,bhsd,bhtsd->bhts', qc, kc, decay)
        o_intra = jnp.einsum('bhts,bhse->bhte', scores, vc)
        b_last = b[:, :, -1:, :]
        S = (jnp.exp(b_last[:, :, 0, :, None]) * S
             + jnp.einsum('bhsd,bhse->bhde', kc * jnp.exp(b_last - b), vc))
        return S, o_inter + o_intra

    S0 = jnp.zeros((B, HG_HEADS, HG_DK, HG_DV), f32)
    _, o = lax.scan(step, S0, (chunks(q, HG_DK), chunks(k, HG_DK), chunks(v, HG_DV), chunks(log_f, HG_DK)))
    o = o.transpose(1, 0, 3, 2, 4).reshape(B, T, HG_HEADS, HG_DV)
    o = rmsnorm(o, g_norm).reshape(B, T, D_MODEL) * jax.nn.sigmoid(g.astype(f32))
    return o.astype(x.dtype) @ w_out


def dsa_mixer(x, w_in, w_out):
    B, T, _ = x.shape
    f32 = jnp.float32
    cuts, acc = [], 0
    for width in DSA_SPLITS[:-1]:
        acc += width
        cuts.append(acc)
    q, k, v, iq, ik, iw = jnp.split(x @ w_in, cuts, axis=-1)
    q = q.reshape(B, T, ATT_HEADS, ATT_HEAD_DIM)
    k = k.reshape(B, T, ATT_KV_HEADS, ATT_HEAD_DIM)
    v = v.reshape(B, T, ATT_KV_HEADS, ATT_HEAD_DIM)
    iq = iq.reshape(B, T, IDX_HEADS, IDX_HEAD_DIM)
    cos, sin = rope_cos_sin(T, ATT_HEAD_DIM)
    q = apply_rope(q, cos, sin)
    k = apply_rope(k, cos, sin)
    icos, isin = rope_cos_sin(T, IDX_ROPE_DIM)
    iq = apply_rope(iq, icos, isin)
    ik = apply_rope(ik[:, :, None, :], icos, isin)[:, :, 0, :]
    iw = iw.astype(f32) * (IDX_HEADS ** -0.5 * IDX_HEAD_DIM ** -0.5)
    n_sel = min(TOPK_MAX, T // 4)
    n_blocks = T // Q_BLOCK
    key_pos = jnp.arange(T, dtype=jnp.int32)

    def to_blocks(t):
        return jnp.swapaxes(t.reshape((B, n_blocks, Q_BLOCK) + t.shape[2:]), 0, 1)

    def block(args):
        qb, iqb, iwb, start = args
        q_pos = start + jnp.arange(Q_BLOCK, dtype=jnp.int32)
        causal = key_pos[None, :] <= q_pos[:, None]
        dots = jnp.einsum('bqhd,bsd->bqhs', iqb, ik, preferred_element_type=f32)
        score = jnp.einsum('bqh,bqhs->bqs', iwb, jax.nn.relu(dots))
        score = jnp.where(causal[None], score, -jnp.inf)
        _, sel = lax.top_k(score, n_sel)
        valid = sel <= q_pos[None, :, None]
        k_sel = jax.vmap(lambda kk, ii: kk[ii])(k, sel)
        v_sel = jax.vmap(lambda vv, ii: vv[ii])(v, sel)
        qg = qb.reshape(B, Q_BLOCK, ATT_KV_HEADS, ATT_GROUP, ATT_HEAD_DIM)
        logits = jnp.einsum('bqhgd,bqnhd->bqhgn', qg, k_sel, preferred_element_type=f32) * (ATT_HEAD_DIM ** -0.5)
        logits = jnp.where(valid[:, :, None, None, :], logits, -jnp.inf)
        probs = jax.nn.softmax(logits, axis=-1)
        o = jnp.einsum('bqhgn,bqnhd->bqhgd', probs.astype(v_sel.dtype), v_sel)
        return o.reshape(B, Q_BLOCK, ATT_HEADS * ATT_HEAD_DIM)

    starts = jnp.arange(n_blocks, dtype=jnp.int32) * Q_BLOCK
    o = lax.map(block, (to_blocks(q), to_blocks(iq), to_blocks(iw), starts))
    o = jnp.swapaxes(o, 0, 1).reshape(B, T, ATT_HEADS * ATT_HEAD_DIM)
    return o @ w_out


def moe_ffn(x, router_w, router_b, w1, b1, w2, b2):
    B, T, D = x.shape
    f32 = jnp.float32
    xt = x.reshape(B * T, D)
    logits = (xt @ router_w).astype(f32) + router_b.astype(f32)
    top_val, top_idx = lax.top_k(logits, MOE_TOP_K)
    gates = jax.nn.softmax(top_val, axis=-1)
    combine = jnp.einsum('nk,nke->ne', gates, jax.nn.one_hot(top_idx, N_EXPERTS, dtype=f32))
    y = jnp.zeros((B * T, D), f32)
    for e in range(N_EXPERTS):
        h = xt @ w1[e] + b1[e]
        gate = jnp.minimum(h[:, :D_EXPERT], SWIGLU_LIMIT)
        up = jnp.clip(h[:, D_EXPERT:], -SWIGLU_LIMIT, SWIGLU_LIMIT)
        glu = gate * jax.nn.sigmoid(SWIGLU_ALPHA * gate)
        out = ((up + 1.0) * glu) @ w2[e] + b2[e]
        y = y + combine[:, e:e + 1] * out.astype(f32)
    return y.astype(x.dtype).reshape(B, T, D)


def _normal(key, shape, scale):
    return jax.random.normal(key, shape, jnp.float32) * scale


def _gain(key, n):
    return 1.0 + 0.02 * jax.random.normal(key, (n,), jnp.float32)


def setup_inputs(seed: int = 0) -> dict:
    key = jax.random.key(seed)
    k_x, k_lb, k_fn, k_layers = jax.random.split(key, 4)
    params = {
        'x': _normal(k_x, (BATCH, SEQ, D_MODEL), 1.0),
        'hgrn_lb_logits': _normal(k_lb, (DEPTH, D_MODEL), 0.1),
    }
    layer_keys = jax.random.split(k_layers, DEPTH)
    for i in range(DEPTH):
        ks = jax.random.split(layer_keys[i], 12)
        p = 'l%d_' % i
        kind = MIXER_PATTERN[i % N_MIXERS]
        params[p + 'norm_mix'] = _gain(ks[0], D_MODEL)
        if kind == 'short_conv':
            params[p + 'conv_in'] = _normal(ks[1], (D_MODEL, 3 * D_MODEL), D_MODEL ** -0.5)
            params[p + 'conv_w'] = _normal(ks[2], (CONV_WIDTH, D_MODEL), CONV_WIDTH ** -0.5)
            params[p + 'conv_out'] = _normal(ks[3], (D_MODEL, D_MODEL), RESID_SCALE * D_MODEL ** -0.5)
        elif kind == 'hgrn2':
            params[p + 'hgrn_in'] = _normal(ks[1], (D_MODEL, 4 * D_MODEL), D_MODEL ** -0.5)
            params[p + 'hgrn_gnorm'] = _gain(ks[2], HG_DV)
            params[p + 'hgrn_out'] = _normal(ks[3], (D_MODEL, D_MODEL), RESID_SCALE * D_MODEL ** -0.5)
        else:
            params[p + 'dsa_in'] = _normal(ks[1], (D_MODEL, DSA_IN_DIM), D_MODEL ** -0.5)
            params[p + 'dsa_out'] = _normal(ks[3], (ATT_HEADS * ATT_HEAD_DIM, D_MODEL),
                                            RESID_SCALE * (ATT_HEADS * ATT_HEAD_DIM) ** -0.5)
        params[p + 'norm_ffn'] = _gain(ks[4], D_MODEL)
        params[p + 'router_w'] = _normal(ks[5], (D_MODEL, N_EXPERTS), D_MODEL ** -0.5)
        params[p + 'router_b'] = _normal(ks[6], (N_EXPERTS,), 0.01)
        params[p + 'exp_w1'] = _normal(ks[7], (N_EXPERTS, D_MODEL, 2 * D_EXPERT), D_MODEL ** -0.5)
        params[p + 'exp_b1'] = _normal(ks[8], (N_EXPERTS, 2 * D_EXPERT), 0.01)
        params[p + 'exp_w2'] = _normal(ks[9], (N_EXPERTS, D_EXPERT, D_MODEL), RESID_SCALE * D_EXPERT ** -0.5)
        params[p + 'exp_b2'] = _normal(ks[10], (N_EXPERTS, D_MODEL), 0.01)
    params['final_norm'] = _gain(k_fn, D_MODEL)
    return params


def reference(x, hgrn_lb_logits,
              l0_norm_mix, l0_conv_in, l0_conv_w, l0_conv_out, l0_norm_ffn,
              l0_router_w, l0_router_b, l0_exp_w1, l0_exp_b1, l0_exp_w2, l0_exp_b2,
              l1_norm_mix, l1_hgrn_in, l1_hgrn_gnorm, l1_hgrn_out, l1_norm_ffn,
              l1_router_w, l1_router_b, l1_exp_w1, l1_exp_b1, l1_exp_w2, l1_exp_b2,
              l2_norm_mix, l2_dsa_in, l2_dsa_out, l2_norm_ffn,
              l2_router_w, l2_router_b, l2_exp_w1, l2_exp_b1, l2_exp_w2, l2_exp_b2,
              l3_norm_mix, l3_conv_in, l3_conv_w, l3_conv_out, l3_norm_ffn,
              l3_router_w, l3_router_b, l3_exp_w1, l3_exp_b1, l3_exp_w2, l3_exp_b2,
              final_norm):
    norm_mix = [l0_norm_mix, l1_norm_mix, l2_norm_mix, l3_norm_mix]
    mixer_args = [
        (l0_conv_in, l0_conv_w, l0_conv_out),
        (l1_hgrn_in, l1_hgrn_gnorm, l1_hgrn_out),
        (l2_dsa_in, l2_dsa_out),
        (l3_conv_in, l3_conv_w, l3_conv_out),
    ]
    norm_ffn = [l0_norm_ffn, l1_norm_ffn, l2_norm_ffn, l3_norm_ffn]
    moe_args = [
        (l0_router_w, l0_router_b, l0_exp_w1, l0_exp_b1, l0_exp_w2, l0_exp_b2),
        (l1_router_w, l1_router_b, l1_exp_w1, l1_exp_b1, l1_exp_w2, l1_exp_b2),
        (l2_router_w, l2_router_b, l2_exp_w1, l2_exp_b1, l2_exp_w2, l2_exp_b2),
        (l3_router_w, l3_router_b, l3_exp_w1, l3_exp_b1, l3_exp_w2, l3_exp_b2),
    ]
    lb_soft = jax.nn.softmax(hgrn_lb_logits.astype(jnp.float32), axis=0)
    lb_all = jnp.cumsum(lb_soft, axis=0) - lb_soft[0:1]

    h = x
    for i in range(DEPTH):
        kind = MIXER_PATTERN[i % N_MIXERS]
        hn = rmsnorm(h, norm_mix[i])
        if kind == 'short_conv':
            mixed = short_conv_mixer(hn, *mixer_args[i])
        elif kind == 'hgrn2':
            mixed = hgrn2_mixer(hn, *mixer_args[i], lb_all[i])
        else:
            mixed = dsa_mixer(hn, *mixer_args[i])
        h = h + mixed
        h = h + moe_ffn(rmsnorm(h, norm_ffn[i]), *moe_args[i])
    return rmsnorm(h, final_norm)
```

```python
import functools

import numpy as np
import jax
import jax.numpy as jnp
from jax import lax
from jax.experimental import pallas as pl
from jax.experimental.pallas import tpu as pltpu

F32 = jnp.float32
BF16 = jnp.bfloat16
I32 = jnp.int32

D_MODEL = 2048
RMS_EPS = 1e-6
ROPE_THETA = 10000.0

HG_HEADS = 16
HG_DK = 128

ATT_HEADS = 16
ATT_KV_HEADS = 4
ATT_GROUP = ATT_HEADS // ATT_KV_HEADS
HEAD_DIM = 128
IDX_HEADS = 16
IDX_ROPE_DIM = 64
TOPK_MAX = 256

N_EXPERTS = 32
MOE_TOP_K = 4
D_EXPERT = D_MODEL // 2
SWIGLU_LIMIT = 7.0
SWIGLU_ALPHA = 1.702

LANES = 128
SUBLANES = 8
VMEM_LIMIT_BYTES = 56 * 1024 * 1024
NEG_BIG = -1e30


def _params(*sem):
    return pltpu.CompilerParams(dimension_semantics=sem, vmem_limit_bytes=VMEM_LIMIT_BYTES)


def _norm_rows(x, g):
    ms = jnp.mean(x * x, axis=-1, keepdims=True)
    return x * lax.rsqrt(ms + RMS_EPS) * g


def _sigmoid(x):
    return 1.0 / (1.0 + jnp.exp(-x))


def _proj_kernel(h_ref, g_ref, w_ref, o_ref, xn_ref):
    @pl.when(pl.program_id(1) == 0)
    def _():
        xn_ref[...] = _norm_rows(h_ref[...], g_ref[...]).astype(BF16)

    o_ref[...] = jnp.dot(xn_ref[...], w_ref[...], preferred_element_type=F32).astype(o_ref.dtype)


def norm_proj(h, gain, w, *, tm=1024, tn=1024, out_dtype=F32):
    T, D = h.shape
    N = w.shape[1]
    tm, tn = min(tm, T), min(tn, N)
    return pl.pallas_call(
        _proj_kernel,
        grid=(T // tm, N // tn),
        in_specs=[
            pl.BlockSpec((tm, D), lambda i, j: (i, 0)),
            pl.BlockSpec((1, D), lambda i, j: (0, 0)),
            pl.BlockSpec((D, tn), lambda i, j: (0, j)),
        ],
        out_specs=pl.BlockSpec((tm, tn), lambda i, j: (i, j)),
        out_shape=jax.ShapeDtypeStruct((T, N), out_dtype),
        scratch_shapes=[pltpu.VMEM((tm, D), BF16)],
        compiler_params=_params("parallel", "arbitrary"),
        name="norm_proj",
    )(h, gain.reshape(1, D), w)


def _matmul_res_kernel(a_ref, w_ref, h_ref, o_ref):
    o_ref[...] = h_ref[...] + jnp.dot(a_ref[...], w_ref[...], preferred_element_type=F32)


def matmul_residual(a, w, h, *, tm=1024, tn=1024):
    T, K = a.shape
    N = w.shape[1]
    tm, tn = min(tm, T), min(tn, N)
    return pl.pallas_call(
        _matmul_res_kernel,
        grid=(T // tm, N // tn),
        in_specs=[
            pl.BlockSpec((tm, K), lambda i, j: (i, 0)),
            pl.BlockSpec((K, tn), lambda i, j: (0, j)),
            pl.BlockSpec((tm, tn), lambda i, j: (i, j)),
        ],
        out_specs=pl.BlockSpec((tm, tn), lambda i, j: (i, j)),
        out_shape=jax.ShapeDtypeStruct((T, N), F32),
        compiler_params=_params("parallel", "arbitrary"),
        name="matmul_residual",
    )(a, w, h)


def _final_norm_kernel(h_ref, g_ref, o_ref):
    o_ref[...] = _norm_rows(h_ref[...], g_ref[...])


def final_norm(h, gain, *, tm=512):
    T, D = h.shape
    return pl.pallas_call(
        _final_norm_kernel,
        grid=(T // tm,),
        in_specs=[pl.BlockSpec((tm, D), lambda i: (i, 0)), pl.BlockSpec((1, D), lambda i: (0, 0))],
        out_specs=pl.BlockSpec((tm, D), lambda i: (i, 0)),
        out_shape=jax.ShapeDtypeStruct((T, D), F32),
        compiler_params=_params("parallel"),
        name="final_norm",
    )(h, gain.reshape(1, D))


def _conv_in_kernel(h_ref, g_ref, wb_ref, wc_ref, wh_ref, b_ref, u_ref, xn_ref):
    @pl.when(pl.program_id(1) == 0)
    def _():
        xn_ref[...] = _norm_rows(h_ref[...], g_ref[...]).astype(BF16)

    xn = xn_ref[...]
    b_ref[...] = jnp.dot(xn, wb_ref[...], preferred_element_type=F32)
    c = jnp.dot(xn, wc_ref[...], preferred_element_type=F32)
    hh = jnp.dot(xn, wh_ref[...], preferred_element_type=F32)
    u_ref[...] = c * hh


def conv_in_proj(h, gain, w_in, *, tm=1024, tn=512):
    T, D = h.shape
    tm = min(tm, T)
    nb = D // tn
    out = jax.ShapeDtypeStruct((T, D), F32)
    return pl.pallas_call(
        _conv_in_kernel,
        grid=(T // tm, nb),
        in_specs=[
            pl.BlockSpec((tm, D), lambda i, j: (i, 0)),
            pl.BlockSpec((1, D), lambda i, j: (0, 0)),
            pl.BlockSpec((D, tn), lambda i, j: (0, j)),
            pl.BlockSpec((D, tn), lambda i, j: (0, j + nb)),
            pl.BlockSpec((D, tn), lambda i, j: (0, j + 2 * nb)),
        ],
        out_specs=[pl.BlockSpec((tm, tn), lambda i, j: (i, j))] * 2,
        out_shape=[out, out],
        scratch_shapes=[pltpu.VMEM((tm, D), BF16)],
        compiler_params=_params("parallel", "arbitrary"),
        name="conv_in_proj",
    )(h, gain.reshape(1, D), w_in, w_in, w_in)


def _shift_rows(u, prev, s):
    r = pltpu.roll(u, s, axis=0)
    p = pltpu.roll(prev, s, axis=0)
    row = lax.broadcasted_iota(I32, p.shape, 0)
    head = jnp.where(row < s, p, r[:SUBLANES])
    return jnp.concatenate([head, r[SUBLANES:]], axis=0)


def _conv_gate_kernel(u_ref, up_ref, b_ref, cw_ref, y_ref):
    u = u_ref[...]
    prev = jnp.where(pl.program_id(0) > 0, up_ref[...], 0.0)
    cw = cw_ref[...]
    conv = cw[2:3] * u + cw[1:2] * _shift_rows(u, prev, 1) + cw[0:1] * _shift_rows(u, prev, 2)
    y_ref[...] = (b_ref[...] * conv).astype(y_ref.dtype)


def conv_gate(u, b, conv_w, *, tm=512, tn=1024):
    T, D = u.shape
    rb = tm // SUBLANES
    return pl.pallas_call(
        _conv_gate_kernel,
        grid=(T // tm, D // tn),
        in_specs=[
            pl.BlockSpec((tm, tn), lambda i, j: (i, j)),
            pl.BlockSpec((SUBLANES, tn), lambda i, j: (jnp.maximum(i * rb - 1, 0), j)),
            pl.BlockSpec((tm, tn), lambda i, j: (i, j)),
            pl.BlockSpec((3, tn), lambda i, j: (0, j)),
        ],
        out_specs=pl.BlockSpec((tm, tn), lambda i, j: (i, j)),
        out_shape=jax.ShapeDtypeStruct((T, D), BF16),
        compiler_params=_params("parallel", "parallel"),
        name="conv_gate",
    )(u, u, b, conv_w)


def short_conv_layer(h, gain, w_in, conv_w, w_out):
    b, u = conv_in_proj(h, gain, w_in)
    y = conv_gate(u, b, conv_w)
    return matmul_residual(y, w_out, h)


HG_ROWS = 512
HG_SUB = 8
HG_LEVELS = 6


def _hgrn_level_table(n):
    tb = np.arange(n)[:, None] // HG_SUB
    sb = np.arange(n)[None, :] // HG_SUB
    x = np.maximum(tb ^ sb, 1)
    lvl = np.floor(np.log2(x)).astype(np.int32)
    return np.where(tb > sb, lvl, -1).astype(np.int32)


def _dot_nt(a, b):
    return lax.dot_general(a, b, (((1,), (1,)), ((), ())), preferred_element_type=F32)


def _hgrn_kernel(q_ref, f_ref, i_ref, g_ref, lb_ref, gn_ref, lvl_ref, o_ref, st_ref, qs, ks, bs, vs, os_):
    R = HG_ROWS

    @pl.when(pl.program_id(1) == 0)
    def _():
        st_ref[...] = jnp.zeros(st_ref.shape, F32)

    lb = lb_ref[...]
    qr = q_ref[...]
    q = qr * _sigmoid(qr)
    forget = lb + (1.0 - lb) * _sigmoid(f_ref[...])
    k = 1.0 - forget
    v = i_ref[...]
    row = lax.broadcasted_iota(I32, (R, LANES), 0)
    b = jnp.log(forget)
    step = 1
    while step < R:
        b = b + jnp.where(row >= step, pltpu.roll(b, step, axis=0), 0.0)
        step *= 2

    st = st_ref[...]
    o = _dot_nt((q * jnp.exp(b)).astype(BF16), st.astype(BF16))

    lvl = lvl_ref[...]
    scores = jnp.zeros((R, R), F32)
    for p in range(HG_LEVELS):
        half = HG_SUB << p
        blk = 2 * half
        bref = jnp.concatenate(
            [jnp.broadcast_to(b[j * blk + half - 1:j * blk + half], (blk, LANES)) for j in range(R // blk)], axis=0)
        qt = (q * jnp.exp(jnp.minimum(b - bref, 0.0))).astype(BF16)
        kt = (k * jnp.exp(jnp.minimum(bref - b, 0.0))).astype(BF16)
        scores = scores + jnp.where(lvl == p, _dot_nt(qt, kt), 0.0)
    o = o + jnp.dot(scores.astype(BF16), v.astype(BF16), preferred_element_type=F32)

    qs[...] = q
    ks[...] = k
    bs[...] = b
    vs[...] = v
    os_[...] = o
    row8 = lax.broadcasted_iota(I32, (HG_SUB, LANES), 0)

    def group(gi, carry):
        r0 = pl.multiple_of(gi * HG_SUB, HG_SUB)
        qb = qs[pl.ds(r0, HG_SUB), :]
        kb = ks[pl.ds(r0, HG_SUB), :]
        bb = bs[pl.ds(r0, HG_SUB), :]
        vb = vs[pl.ds(r0, HG_SUB), :]
        acc = jnp.zeros((HG_SUB, LANES), F32)
        for s in range(HG_SUB):
            decay = jnp.exp(jnp.where(row8 >= s, bb - bb[s:s + 1], -jnp.inf))
            a = jnp.sum(qb * kb[s:s + 1] * decay, axis=-1, keepdims=True)
            acc = acc + a * vb[s:s + 1]
        os_[pl.ds(r0, HG_SUB), :] += acc
        return carry

    lax.fori_loop(0, R // HG_SUB, group, 0, unroll=8)

    b_last = b[R - 1:R]
    kd = (k * jnp.exp(b_last - b)).astype(BF16)
    st_ref[...] = st * jnp.exp(b_last) + jnp.dot(v.T.astype(BF16), kd, preferred_element_type=F32)

    o = os_[...]
    o = o * lax.rsqrt(jnp.mean(o * o, axis=-1, keepdims=True) + RMS_EPS) * gn_ref[...]
    o_ref[...] = (o * _sigmoid(g_ref[...])).astype(o_ref.dtype)


def hgrn2_core(z, lb, gnorm):
    T = z.shape[0]
    D = z.shape[1] // 4
    R = HG_ROWS
    nh = D // LANES
    lvl = jnp.asarray(_hgrn_level_table(R))
    blk = lambda off: pl.BlockSpec((R, LANES), lambda hd, r: (r, off * nh + hd))
    return pl.pallas_call(
        _hgrn_kernel,
        grid=(nh, T // R),
        in_specs=[
            blk(0), blk(1), blk(2), blk(3),
            pl.BlockSpec((1, LANES), lambda hd, r: (0, hd)),
            pl.BlockSpec((1, LANES), lambda hd, r: (0, 0)),
            pl.BlockSpec((R, R), lambda hd, r: (0, 0)),
        ],
        out_specs=pl.BlockSpec((R, LANES), lambda hd, r: (r, hd)),
        out_shape=jax.ShapeDtypeStruct((T, D), BF16),
        scratch_shapes=[pltpu.VMEM((LANES, LANES), F32)] + [pltpu.VMEM((R, LANES), F32)] * 5,
        compiler_params=_params("parallel", "arbitrary"),
        name="hgrn2_core",
    )(z, z, z, z, lb.reshape(1, D), gnorm.reshape(1, LANES), lvl)


def hgrn2_layer(h, gain, w_in, gnorm, w_out, lb):
    z = norm_proj(h, gain, w_in)
    og = hgrn2_core(z, lb, gnorm)
    return matmul_residual(og, w_out, h)


DSA_QB = 128
DSA_KC = 512
DSA_KB = DSA_KC // LANES
INT32_MIN = -(2 ** 31)


def _rope_tables(T, rot_dim):
    half = rot_dim // 2
    inv_freq = 1.0 / (ROPE_THETA ** (jnp.arange(0, rot_dim, 2, dtype=F32) / rot_dim))
    ang = jnp.arange(T, dtype=F32)[:, None] * inv_freq[None, :]
    cos, sin = jnp.cos(ang), jnp.sin(ang)
    ones = jnp.ones((T, HEAD_DIM - rot_dim), F32)
    zeros_h = jnp.zeros((T, half), F32)
    zeros_r = jnp.zeros((T, HEAD_DIM - rot_dim), F32)
    c = jnp.concatenate([cos, cos, ones], axis=1)
    s_lo = jnp.concatenate([-sin, zeros_h, zeros_r], axis=1)
    s_hi = jnp.concatenate([zeros_h, sin, zeros_r], axis=1)
    return c, s_lo, s_hi


def _rope_proj_kernel(h_ref, g_ref, w_ref, c_ref, slo_ref, shi_ref, o_ref, xn_ref, *, half, scale, n_scaled):
    @pl.when(pl.program_id(1) == 0)
    def _():
        xn_ref[...] = _norm_rows(h_ref[...], g_ref[...]).astype(BF16)

    y = jnp.dot(xn_ref[...], w_ref[...], preferred_element_type=F32)
    y = y * jnp.where(pl.program_id(1) < n_scaled, scale, 1.0)
    c, slo, shi = c_ref[...], slo_ref[...], shi_ref[...]
    n_rb, hb = o_ref.shape[0], o_ref.shape[1]
    for hh in range(hb):
        yh = y[:, hh * LANES:(hh + 1) * LANES]
        if 2 * half == LANES:
            r = yh * c + pltpu.roll(yh, half, axis=1) * (slo + shi)
        else:
            r = yh * c + pltpu.roll(yh, LANES - half, axis=1) * slo + pltpu.roll(yh, half, axis=1) * shi
        r = r.astype(o_ref.dtype)
        for rb in range(n_rb):
            o_ref[rb, hh] = r[rb * LANES:(rb + 1) * LANES]


def rope_proj(h, gain, w, tables, *, rot_dim, scale=1.0, n_scaled=0, tm=1024, hb=4):
    T, D = h.shape
    n_heads = w.shape[1] // LANES
    tm, hb = min(tm, T), min(hb, n_heads)
    tn = hb * LANES
    tbl = pl.BlockSpec((tm, LANES), lambda i, j: (i, 0))
    return pl.pallas_call(
        functools.partial(_rope_proj_kernel, half=rot_dim // 2, scale=scale, n_scaled=n_scaled),
        grid=(T // tm, n_heads // hb),
        in_specs=[
            pl.BlockSpec((tm, D), lambda i, j: (i, 0)),
            pl.BlockSpec((1, D), lambda i, j: (0, 0)),
            pl.BlockSpec((D, tn), lambda i, j: (0, j)),
            tbl, tbl, tbl,
        ],
        out_specs=pl.BlockSpec((tm // LANES, hb, LANES, LANES), lambda i, j: (i, j, 0, 0)),
        out_shape=jax.ShapeDtypeStruct((T // LANES, n_heads, LANES, LANES), BF16),
        scratch_shapes=[pltpu.VMEM((tm, D), BF16)],
        compiler_params=_params("parallel", "arbitrary"),
        name="rope_proj",
    )(h, gain.reshape(1, D), w, *tables)


def _f32_from_ordered_bits(key):
    return pltpu.bitcast(key ^ ((key >> 31) & 0x7FFFFFFF), F32)


def _dsa_kernel(q_ref, iq_ref, iw_ref, k_ref, v_ref, ik_ref, o_ref, score_ref, ot_ref, m_ref, s_ref, *, n_sel):
    i = pl.program_id(0)
    n_chunks = (i * DSA_QB + DSA_QB + DSA_KC - 1) // DSA_KC
    q_pos = i * DSA_QB + lax.broadcasted_iota(I32, (DSA_KC, LANES), 1)
    row_iota = lax.broadcasted_iota(I32, (DSA_KC, LANES), 0)

    iw_t = iw_ref[...].T
    iq2 = iq_ref[0].reshape(IDX_HEADS * DSA_QB, HEAD_DIM)

    def index_chunk(c, carry):
        ikc = ik_ref[pl.ds(c * DSA_KB, DSA_KB), 0].reshape(DSA_KC, HEAD_DIM)
        d = _dot_nt(ikc, iq2)
        acc = jnp.zeros((DSA_KC, LANES), F32)
        for hh in range(IDX_HEADS):
            acc = acc + iw_t[hh:hh + 1, :] * jnp.maximum(d[:, hh * LANES:(hh + 1) * LANES], 0.0)
        causal = c * DSA_KC + row_iota <= q_pos
        score_ref[pl.ds(pl.multiple_of(c * DSA_KC, DSA_KC), DSA_KC), :] = jnp.where(causal, acc, -jnp.inf)
        return carry

    lax.fori_loop(0, n_chunks, index_chunk, 0)

    def count_ge(cand):
        def body(c, acc):
            sc = score_ref[pl.ds(pl.multiple_of(c * DSA_KC, DSA_KC), DSA_KC), :]
            hit = jnp.where(sc >= cand, 1, 0).astype(I32)
            return acc + jnp.sum(hit.reshape(DSA_KC // SUBLANES, SUBLANES, LANES), axis=0)

        acc = lax.fori_loop(0, n_chunks, body, jnp.zeros((SUBLANES, LANES), I32))
        return jnp.sum(acc, axis=0, keepdims=True)

    thr_key = jnp.where(count_ge(jnp.zeros((1, LANES), F32)) >= n_sel, 0, INT32_MIN).astype(I32)

    def bit_step(bi, thr_key):
        cand = thr_key + jnp.left_shift(jnp.int32(1), 30 - bi)
        return jnp.where(count_ge(_f32_from_ordered_bits(cand)) >= n_sel, cand, thr_key)

    thr_key = lax.fori_loop(0, 31, bit_step, thr_key)
    thr = jnp.where(thr_key == INT32_MIN, -jnp.inf, _f32_from_ordered_bits(thr_key))

    m_ref[...] = jnp.full(m_ref.shape, -1e29, F32)
    s_ref[...] = jnp.zeros(s_ref.shape, F32)
    ot_ref[...] = jnp.zeros(ot_ref.shape, F32)

    def attend_chunk(c, carry):
        r0 = pl.multiple_of(c * DSA_KC, DSA_KC)
        sc = score_ref[pl.ds(r0, DSA_KC), :]
        keep = jnp.logical_and(sc >= thr, c * DSA_KC + row_iota <= q_pos)
        bias = jnp.where(keep, 0.0, NEG_BIG)
        bias4 = jnp.concatenate([bias] * ATT_GROUP, axis=1)
        for g in range(ATT_KV_HEADS):
            kc = k_ref[pl.ds(c * DSA_KB, DSA_KB), g].reshape(DSA_KC, HEAD_DIM)
            vc = v_ref[pl.ds(c * DSA_KB, DSA_KB), g].reshape(DSA_KC, HEAD_DIM)
            qg = q_ref[0, g * ATT_GROUP:(g + 1) * ATT_GROUP].reshape(ATT_GROUP * DSA_QB, HEAD_DIM)
            logit = _dot_nt(kc, qg) + bias4
            m_old = m_ref[g]
            m_new = jnp.maximum(m_old, jnp.max(logit, axis=0, keepdims=True))
            alpha = jnp.exp(m_old - m_new)
            p = jnp.exp(logit - m_new)
            s_ref[g] = alpha * s_ref[g] + jnp.sum(p, axis=0, keepdims=True)
            pv = lax.dot_general(vc, p.astype(BF16), (((0,), (0,)), ((), ())), preferred_element_type=F32)
            ot_ref[g] = ot_ref[g] * alpha + pv
            m_ref[g] = m_new
        return carry

    lax.fori_loop(0, n_chunks, attend_chunk, 0)

    for g in range(ATT_KV_HEADS):
        o_t = ot_ref[g] * (1.0 / s_ref[g])
        for hh in range(ATT_GROUP):
            head = g * ATT_GROUP + hh
            o_ref[:, head * HEAD_DIM:(head + 1) * HEAD_DIM] = o_t[:, hh * LANES:(hh + 1) * LANES].T.astype(o_ref.dtype)


def dsa_core(qk, v, iq, ik, iw, n_sel):
    nb = qk.shape[0]
    T = nb * LANES
    resident = lambda heads, first: pl.BlockSpec((nb, heads, LANES, LANES), lambda i: (0, first // heads, 0, 0),
                                                 pipeline_mode=pl.Buffered(1))
    return pl.pallas_call(
        functools.partial(_dsa_kernel, n_sel=n_sel),
        grid=(nb,),
        in_specs=[
            pl.BlockSpec((1, ATT_HEADS, LANES, LANES), lambda i: (i, 0, 0, 0)),
            pl.BlockSpec((1, IDX_HEADS, LANES, LANES), lambda i: (i, 0, 0, 0)),
            pl.BlockSpec((DSA_QB, LANES), lambda i: (i, 0)),
            resident(ATT_KV_HEADS, ATT_HEADS),
            resident(ATT_KV_HEADS, 0),
            resident(1, 0),
        ],
        out_specs=pl.BlockSpec((DSA_QB, ATT_HEADS * HEAD_DIM), lambda i: (i, 0)),
        out_shape=jax.ShapeDtypeStruct((T, ATT_HEADS * HEAD_DIM), BF16),
        scratch_shapes=[
            pltpu.VMEM((T, LANES), F32),
            pltpu.VMEM((ATT_KV_HEADS, HEAD_DIM, ATT_GROUP * DSA_QB), F32),
            pltpu.VMEM((ATT_KV_HEADS, 1, ATT_GROUP * DSA_QB), F32),
            pltpu.VMEM((ATT_KV_HEADS, 1, ATT_GROUP * DSA_QB), F32),
        ],
        compiler_params=_params("parallel"),
        name="dsa_core",
    )(qk, iq, iw, qk, v, ik)


def dsa_layer(h, gain, w_in, w_out):
    T, D = h.shape
    nq, nkv = ATT_HEADS * HEAD_DIM, ATT_KV_HEADS * HEAD_DIM
    ni = IDX_HEADS * HEAD_DIM
    w_qk = w_in[:, :nq + nkv].astype(BF16)
    w_v = w_in[:, nq + nkv:nq + 2 * nkv].astype(BF16)
    o_i = nq + 2 * nkv
    w_iq = w_in[:, o_i:o_i + ni].astype(BF16)
    w_ik = w_in[:, o_i + ni:o_i + ni + HEAD_DIM].astype(BF16)
    w_iw = jnp.zeros((D, LANES), F32).at[:, :IDX_HEADS].set(w_in[:, o_i + ni + HEAD_DIM:]).astype(BF16)
    qk = rope_proj(h, gain, w_qk, _rope_tables(T, HEAD_DIM), rot_dim=HEAD_DIM, scale=HEAD_DIM ** -0.5,
                   n_scaled=ATT_HEADS // 4)
    idx_tables = _rope_tables(T, IDX_ROPE_DIM)
    iq = rope_proj(h, gain, w_iq, idx_tables, rot_dim=IDX_ROPE_DIM)
    ik = rope_proj(h, gain, w_ik, idx_tables, rot_dim=IDX_ROPE_DIM)
    v = norm_proj(h, gain, w_v, out_dtype=BF16).reshape(T // LANES, LANES, ATT_KV_HEADS, HEAD_DIM).transpose(0, 2, 1, 3)
    iw = norm_proj(h, gain, w_iw) * (IDX_HEADS ** -0.5 * HEAD_DIM ** -0.5)
    o = dsa_core(qk, v, iq, ik, iw, min(TOPK_MAX, T // 4))
    return matmul_residual(o, w_out, h)


MOE_TILE = 256
MOE_EXPERT_VMEM_LIMIT_BYTES = 60 * 1024 * 1024


def _router_kernel(h_ref, g_ref, rw_ref, rb_ref, xn_ref, idx_ref, gate_ref):
    xn = _norm_rows(h_ref[...], g_ref[...])
    xn_ref[...] = xn
    logits = jnp.dot(xn, rw_ref[...], preferred_element_type=F32, precision=lax.Precision.HIGHEST) + rb_ref[...]
    lane = lax.broadcasted_iota(I32, logits.shape, 1)
    vals, idxs = [], []
    for _ in range(MOE_TOP_K):
        m = jnp.max(logits, axis=-1, keepdims=True)
        sel = jnp.min(jnp.where(logits == m, lane, LANES), axis=-1, keepdims=True)
        vals.append(m)
        idxs.append(sel)
        logits = jnp.where(lane == sel, -jnp.inf, logits)
    ex = [jnp.exp(v - vals[0]) for v in vals]
    den = ex[0] + ex[1] + ex[2] + ex[3]
    idx_out = jnp.zeros(lane.shape, I32)
    gate_out = jnp.zeros(lane.shape, F32)
    for k in range(MOE_TOP_K):
        idx_out = jnp.where(lane == k, idxs[k], idx_out)
        gate_out = jnp.where(lane == k, ex[k] / den, gate_out)
    idx_ref[...] = idx_out
    gate_ref[...] = gate_out


def moe_router(h, gain, router_w, router_b, *, tm=256):
    T, D = h.shape
    rw = jnp.zeros((D, LANES), F32).at[:, :N_EXPERTS].set(router_w)
    rb = jnp.full((1, LANES), -jnp.inf, F32).at[0, :N_EXPERTS].set(router_b)
    return pl.pallas_call(
        _router_kernel,
        grid=(T // tm,),
        in_specs=[
            pl.BlockSpec((tm, D), lambda i: (i, 0)),
            pl.BlockSpec((1, D), lambda i: (0, 0)),
            pl.BlockSpec((D, LANES), lambda i: (0, 0)),
            pl.BlockSpec((1, LANES), lambda i: (0, 0)),
        ],
        out_specs=[
            pl.BlockSpec((tm, D), lambda i: (i, 0)),
            pl.BlockSpec((tm, LANES), lambda i: (i, 0)),
            pl.BlockSpec((tm, LANES), lambda i: (i, 0)),
        ],
        out_shape=[
            jax.ShapeDtypeStruct((T, D), F32),
            jax.ShapeDtypeStruct((T, LANES), I32),
            jax.ShapeDtypeStruct((T, LANES), F32),
        ],
        compiler_params=_params("parallel"),
        name="moe_router",
    )(h, gain.reshape(1, D), rw, rb)


def _route(idx, n_slots):
    T = idx.shape[0]
    n_tiles = n_slots // MOE_TILE
    flat_e = idx.reshape(-1)
    onehot = (flat_e[:, None] == jnp.arange(N_EXPERTS, dtype=I32)[None, :]).astype(I32)
    csum = jnp.cumsum(onehot, axis=0)
    rank = jnp.sum(csum * onehot, axis=1) - 1
    counts = csum[-1]
    padded = ((counts + MOE_TILE - 1) // MOE_TILE) * MOE_TILE
    ends = jnp.cumsum(padded)
    starts = ends - padded
    slot = jnp.sum(starts[None, :] * onehot, axis=1) + rank
    token = jnp.arange(T * MOE_TOP_K, dtype=I32) // MOE_TOP_K
    tok_of_slot = jnp.zeros((n_slots,), I32).at[slot].set(token, unique_indices=True)
    tile_start = jnp.arange(n_tiles, dtype=I32) * MOE_TILE
    tile_expert = jnp.sum((tile_start[:, None] >= ends[None, :]).astype(I32), axis=1)
    tile_expert = jnp.minimum(tile_expert, N_EXPERTS - 1)
    n_used = ends[-1] // MOE_TILE
    return tok_of_slot, slot.reshape(T, MOE_TOP_K), tile_expert, n_used.reshape(1)


def _start_row_gather(ids_ref, n_rows, src_hbm, dst_ref, sem, *, both_queues=False):
    for r in range(n_rows):
        row = ids_ref[0, 0, r]
        copy = pltpu.make_async_copy(src_hbm.at[pl.ds(row, 1), :], dst_ref.at[pl.ds(r, 1), :], sem)
        copy.start(priority=r % 2 if both_queues else 0)


def _wait_row_gather(n_rows, src_hbm, dst_ref, sem):
    pltpu.make_async_copy(src_hbm.at[pl.ds(0, n_rows), :], dst_ref, sem).wait()


def _expert_kernel(texp_ref, nused_ref, ids_cur_ref, ids_nxt_ref, x_hbm, w1_ref, b1_ref, w2_ref, b2_ref,
                   y_ref, xbuf, sem):
    t = pl.program_id(0)
    n_used = nused_ref[0]
    slot = t % 2

    @pl.when(jnp.logical_and(t == 0, n_used > 0))
    def _():
        _start_row_gather(ids_cur_ref, MOE_TILE, x_hbm, xbuf.at[0], sem.at[0])

    @pl.when(t + 1 < n_used)
    def _():
        _start_row_gather(ids_nxt_ref, MOE_TILE, x_hbm, xbuf.at[1 - slot], sem.at[1 - slot])

    @pl.when(t < n_used)
    def _():
        _wait_row_gather(MOE_TILE, x_hbm, xbuf.at[slot], sem.at[slot])
        x = xbuf[slot].astype(BF16)
        hcat = jnp.dot(x, w1_ref[0].astype(BF16), preferred_element_type=F32) + b1_ref[0]
        gate = jnp.minimum(hcat[:, :D_EXPERT], SWIGLU_LIMIT)
        up = jnp.clip(hcat[:, D_EXPERT:], -SWIGLU_LIMIT, SWIGLU_LIMIT)
        glu = gate * _sigmoid(SWIGLU_ALPHA * gate)
        act = ((up + 1.0) * glu).astype(BF16)
        y_ref[...] = jnp.dot(act, w2_ref[0].astype(BF16), preferred_element_type=F32) + b2_ref[0]

    @pl.when(t >= n_used)
    def _():
        y_ref[...] = jnp.zeros(y_ref.shape, F32)


def moe_experts(xn, tok_of_slot, tile_expert, n_used, w1, b1, w2, b2):
    T, D = xn.shape
    n_slots = tok_of_slot.shape[0]
    n_tiles = n_slots // MOE_TILE
    ids = tok_of_slot.reshape(n_tiles, 1, MOE_TILE)
    grid_spec = pltpu.PrefetchScalarGridSpec(
        num_scalar_prefetch=2,
        grid=(n_tiles,),
        in_specs=[
            pl.BlockSpec((1, 1, MOE_TILE), lambda t, te, nu: (t, 0, 0), memory_space=pltpu.SMEM),
            pl.BlockSpec((1, 1, MOE_TILE), lambda t, te, nu: (jnp.minimum(t + 1, n_tiles - 1), 0, 0),
                         memory_space=pltpu.SMEM),
            pl.BlockSpec(memory_space=pl.ANY),
            pl.BlockSpec((1, D, 2 * D_EXPERT), lambda t, te, nu: (te[t], 0, 0)),
            pl.BlockSpec((1, 1, 2 * D_EXPERT), lambda t, te, nu: (te[t], 0, 0)),
            pl.BlockSpec((1, D_EXPERT, D), lambda t, te, nu: (te[t], 0, 0)),
            pl.BlockSpec((1, 1, D), lambda t, te, nu: (te[t], 0, 0)),
        ],
        out_specs=pl.BlockSpec((MOE_TILE, D), lambda t, te, nu: (t, 0)),
        scratch_shapes=[pltpu.VMEM((2, MOE_TILE, D), F32), pltpu.SemaphoreType.DMA((2,))],
    )
    return pl.pallas_call(
        _expert_kernel,
        grid_spec=grid_spec,
        out_shape=jax.ShapeDtypeStruct((n_slots, D), F32),
        compiler_params=pltpu.CompilerParams(dimension_semantics=("arbitrary",),
                                             vmem_limit_bytes=MOE_EXPERT_VMEM_LIMIT_BYTES),
        name="moe_experts",
    )(tile_expert, n_used, ids, ids, xn, w1, b1.reshape(N_EXPERTS, 1, -1), w2, b2.reshape(N_EXPERTS, 1, -1))


COMBINE_TOKENS = 128


def _combine_kernel(ids_cur_ref, ids_nxt_ref, y_hbm, gate_ref, h_ref, o_ref, ybuf, sem):
    t = pl.program_id(0)
    n = pl.num_programs(0)
    slot = t % 2
    rows = COMBINE_TOKENS * MOE_TOP_K

    @pl.when(t == 0)
    def _():
        _start_row_gather(ids_cur_ref, rows, y_hbm, ybuf.at[0], sem.at[0], both_queues=True)

    @pl.when(t + 1 < n)
    def _():
        _start_row_gather(ids_nxt_ref, rows, y_hbm, ybuf.at[1 - slot], sem.at[1 - slot], both_queues=True)

    _wait_row_gather(rows, y_hbm, ybuf.at[slot], sem.at[slot])
    gates = gate_ref[...]
    acc = h_ref[...]
    for k in range(MOE_TOP_K):
        acc = acc + gates[:, k:k + 1] * ybuf[slot, pl.ds(k * COMBINE_TOKENS, COMBINE_TOKENS), :]
    o_ref[...] = acc


def moe_combine(y_sorted, slot_of, gate_pad, h):
    T, D = h.shape
    n_steps = T // COMBINE_TOKENS
    rows = COMBINE_TOKENS * MOE_TOP_K
    ids = slot_of.reshape(n_steps, COMBINE_TOKENS, MOE_TOP_K).transpose(0, 2, 1).reshape(n_steps, 1, rows)
    return pl.pallas_call(
        _combine_kernel,
        grid=(n_steps,),
        in_specs=[
            pl.BlockSpec((1, 1, rows), lambda t: (t, 0, 0), memory_space=pltpu.SMEM),
            pl.BlockSpec((1, 1, rows), lambda t: (jnp.minimum(t + 1, n_steps - 1), 0, 0), memory_space=pltpu.SMEM),
            pl.BlockSpec(memory_space=pl.ANY),
            pl.BlockSpec((COMBINE_TOKENS, LANES), lambda t: (t, 0)),
            pl.BlockSpec((COMBINE_TOKENS, D), lambda t: (t, 0)),
        ],
        out_specs=pl.BlockSpec((COMBINE_TOKENS, D), lambda t: (t, 0)),
        out_shape=jax.ShapeDtypeStruct((T, D), F32),
        scratch_shapes=[pltpu.VMEM((2, rows, D), F32), pltpu.SemaphoreType.DMA((2,))],
        compiler_params=_params("arbitrary"),
        name="moe_combine",
    )(ids, ids, y_sorted, gate_pad, h)


def moe_layer(h, gain, router_w, router_b, w1, b1, w2, b2):
    T = h.shape[0]
    n_slots = T * MOE_TOP_K + N_EXPERTS * MOE_TILE
    xn, idx_pad, gate_pad = moe_router(h, gain, router_w, router_b)
    tok_of_slot, slot_of, tile_expert, n_used = _route(idx_pad[:, :MOE_TOP_K], n_slots)
    y_sorted = moe_experts(xn, tok_of_slot, tile_expert, n_used, w1, b1, w2, b2)
    return moe_combine(y_sorted, slot_of, gate_pad, h)


def kernel(x, hgrn_lb_logits,
           l0_norm_mix, l0_conv_in, l0_conv_w, l0_conv_out, l0_norm_ffn,
           l0_router_w, l0_router_b, l0_exp_w1, l0_exp_b1, l0_exp_w2, l0_exp_b2,
           l1_norm_mix, l1_hgrn_in, l1_hgrn_gnorm, l1_hgrn_out, l1_norm_ffn,
           l1_router_w, l1_router_b, l1_exp_w1, l1_exp_b1, l1_exp_w2, l1_exp_b2,
           l2_norm_mix, l2_dsa_in, l2_dsa_out, l2_norm_ffn,
           l2_router_w, l2_router_b, l2_exp_w1, l2_exp_b1, l2_exp_w2, l2_exp_b2,
           l3_norm_mix, l3_conv_in, l3_conv_w, l3_conv_out, l3_norm_ffn,
           l3_router_w, l3_router_b, l3_exp_w1, l3_exp_b1, l3_exp_w2, l3_exp_b2,
           final_norm_gain):
    B, T, D = x.shape
    bf = lambda w: w.astype(BF16)
    h = x.reshape(B * T, D)

    h = short_conv_layer(h, l0_norm_mix, bf(l0_conv_in), l0_conv_w, bf(l0_conv_out))
    h = moe_layer(h, l0_norm_ffn, l0_router_w, l0_router_b, l0_exp_w1, l0_exp_b1, l0_exp_w2, l0_exp_b2)

    lb_soft = jax.nn.softmax(hgrn_lb_logits.astype(F32), axis=0)
    lb_all = jnp.cumsum(lb_soft, axis=0) - lb_soft[0:1]
    h = hgrn2_layer(h, l1_norm_mix, bf(l1_hgrn_in), l1_hgrn_gnorm, bf(l1_hgrn_out), lb_all[1])
    h = moe_layer(h, l1_norm_ffn, l1_router_w, l1_router_b, l1_exp_w1, l1_exp_b1, l1_exp_w2, l1_exp_b2)

    h = dsa_layer(h, l2_norm_mix, l2_dsa_in, bf(l2_dsa_out))
    h = moe_layer(h, l2_norm_ffn, l2_router_w, l2_router_b, l2_exp_w1, l2_exp_b1, l2_exp_w2, l2_exp_b2)

    h = short_conv_layer(h, l3_norm_mix, bf(l3_conv_in), l3_conv_w, bf(l3_conv_out))
    h = moe_layer(h, l3_norm_ffn, l3_router_w, l3_router_b, l3_exp_w1, l3_exp_b1, l3_exp_w2, l3_exp_b2)

    return final_norm(h, final_norm_gain).reshape(B, T, D)
```

```python
import functools

import numpy as np
import jax
import jax.numpy as jnp
from jax import lax
from jax.experimental import pallas as pl
from jax.experimental.pallas import tpu as pltpu

F32 = jnp.float32
BF16 = jnp.bfloat16
I32 = jnp.int32

D_MODEL = 2048
RMS_EPS = 1e-6
ROPE_THETA = 10000.0

HG_HEADS = 16
HG_DK = 128

ATT_HEADS = 16
ATT_KV_HEADS = 4
ATT_GROUP = ATT_HEADS // ATT_KV_HEADS
HEAD_DIM = 128
IDX_HEADS = 16
IDX_ROPE_DIM = 64
TOPK_MAX = 256

N_EXPERTS = 32
MOE_TOP_K = 4
D_EXPERT = D_MODEL // 2
SWIGLU_LIMIT = 7.0
SWIGLU_ALPHA = 1.702

LANES = 128
SUBLANES = 8
VMEM_LIMIT_BYTES = 56 * 1024 * 1024
NEG_BIG = -1e30


def _params(*sem):
    return pltpu.CompilerParams(dimension_semantics=sem, vmem_limit_bytes=VMEM_LIMIT_BYTES)


def _norm_rows(x, g):
    ms = jnp.mean(x * x, axis=-1, keepdims=True)
    return x * lax.rsqrt(ms + RMS_EPS) * g


def _sigmoid(x):
    return 1.0 / (1.0 + jnp.exp(-x))


def _proj_kernel(h_ref, g_ref, w_ref, o_ref, xn_ref):
    @pl.when(pl.program_id(1) == 0)
    def _():
        xn_ref[...] = _norm_rows(h_ref[...], g_ref[...]).astype(BF16)

    o_ref[...] = jnp.dot(xn_ref[...], w_ref[...], preferred_element_type=F32).astype(o_ref.dtype)


def norm_proj(h, gain, w, *, tm=1024, tn=1024, out_dtype=F32):
    T, D = h.shape
    N = w.shape[1]
    tm, tn = min(tm, T), min(tn, N)
    return pl.pallas_call(
        _proj_kernel,
        grid=(T // tm, N // tn),
        in_specs=[
            pl.BlockSpec((tm, D), lambda i, j: (i, 0)),
            pl.BlockSpec((1, D), lambda i, j: (0, 0)),
            pl.BlockSpec((D, tn), lambda i, j: (0, j)),
        ],
        out_specs=pl.BlockSpec((tm, tn), lambda i, j: (i, j)),
        out_shape=jax.ShapeDtypeStruct((T, N), out_dtype),
        scratch_shapes=[pltpu.VMEM((tm, D), BF16)],
        compiler_params=_params("parallel", "arbitrary"),
        name="norm_proj",
    )(h, gain.reshape(1, D), w)


def _matmul_res_kernel(a_ref, w_ref, h_ref, o_ref):
    o_ref[...] = h_ref[...] + jnp.dot(a_ref[...], w_ref[...], preferred_element_type=F32)


def matmul_residual(a, w, h, *, tm=1024, tn=1024):
    T, K = a.shape
    N = w.shape[1]
    tm, tn = min(tm, T), min(tn, N)
    return pl.pallas_call(
        _matmul_res_kernel,
        grid=(T // tm, N // tn),
        in_specs=[
            pl.BlockSpec((tm, K), lambda i, j: (i, 0)),
            pl.BlockSpec((K, tn), lambda i, j: (0, j)),
            pl.BlockSpec((tm, tn), lambda i, j: (i, j)),
        ],
        out_specs=pl.BlockSpec((tm, tn), lambda i, j: (i, j)),
        out_shape=jax.ShapeDtypeStruct((T, N), F32),
        compiler_params=_params("parallel", "arbitrary"),
        name="matmul_residual",
    )(a, w, h)


def _final_norm_kernel(h_ref, g_ref, o_ref):
    o_ref[...] = _norm_rows(h_ref[...], g_ref[...])


def final_norm(h, gain, *, tm=512):
    T, D = h.shape
    return pl.pallas_call(
        _final_norm_kernel,
        grid=(T // tm,),
        in_specs=[pl.BlockSpec((tm, D), lambda i: (i, 0)), pl.BlockSpec((1, D), lambda i: (0, 0))],
        out_specs=pl.BlockSpec((tm, D), lambda i: (i, 0)),
        out_shape=jax.ShapeDtypeStruct((T, D), F32),
        compiler_params=_params("parallel"),
        name="final_norm",
    )(h, gain.reshape(1, D))


def _conv_in_kernel(h_ref, g_ref, wb_ref, wc_ref, wh_ref, b_ref, u_ref, xn_ref):
    @pl.when(pl.program_id(1) == 0)
    def _():
        xn_ref[...] = _norm_rows(h_ref[...], g_ref[...]).astype(BF16)

    xn = xn_ref[...]
    b_ref[...] = jnp.dot(xn, wb_ref[...], preferred_element_type=F32)
    c = jnp.dot(xn, wc_ref[...], preferred_element_type=F32)
    hh = jnp.dot(xn, wh_ref[...], preferred_element_type=F32)
    u_ref[...] = c * hh


def conv_in_proj(h, gain, w_in, *, tm=1024, tn=512):
    T, D = h.shape
    tm = min(tm, T)
    nb = D // tn
    out = jax.ShapeDtypeStruct((T, D), F32)
    return pl.pallas_call(
        _conv_in_kernel,
        grid=(T // tm, nb),
        in_specs=[
            pl.BlockSpec((tm, D), lambda i, j: (i, 0)),
            pl.BlockSpec((1, D), lambda i, j: (0, 0)),
            pl.BlockSpec((D, tn), lambda i, j: (0, j)),
            pl.BlockSpec((D, tn), lambda i, j: (0, j + nb)),
            pl.BlockSpec((D, tn), lambda i, j: (0, j + 2 * nb)),
        ],
        out_specs=[pl.BlockSpec((tm, tn), lambda i, j: (i, j))] * 2,
        out_shape=[out, out],
        scratch_shapes=[pltpu.VMEM((tm, D), BF16)],
        compiler_params=_params("parallel", "arbitrary"),
        name="conv_in_proj",
    )(h, gain.reshape(1, D), w_in, w_in, w_in)


def _shift_rows(u, prev, s):
    r = pltpu.roll(u, s, axis=0)
    p = pltpu.roll(prev, s, axis=0)
    row = lax.broadcasted_iota(I32, p.shape, 0)
    head = jnp.where(row < s, p, r[:SUBLANES])
    return jnp.concatenate([head, r[SUBLANES:]], axis=0)


def _conv_gate_kernel(u_ref, up_ref, b_ref, cw_ref, y_ref):
    u = u_ref[...]
    prev = jnp.where(pl.program_id(0) > 0, up_ref[...], 0.0)
    cw = cw_ref[...]
    conv = cw[2:3] * u + cw[1:2] * _shift_rows(u, prev, 1) + cw[0:1] * _shift_rows(u, prev, 2)
    y_ref[...] = (b_ref[...] * conv).astype(y_ref.dtype)


def conv_gate(u, b, conv_w, *, tm=512, tn=1024):
    T, D = u.shape
    rb = tm // SUBLANES
    return pl.pallas_call(
        _conv_gate_kernel,
        grid=(T // tm, D // tn),
        in_specs=[
            pl.BlockSpec((tm, tn), lambda i, j: (i, j)),
            pl.BlockSpec((SUBLANES, tn), lambda i, j: (jnp.maximum(i * rb - 1, 0), j)),
            pl.BlockSpec((tm, tn), lambda i, j: (i, j)),
            pl.BlockSpec((3, tn), lambda i, j: (0, j)),
        ],
        out_specs=pl.BlockSpec((tm, tn), lambda i, j: (i, j)),
        out_shape=jax.ShapeDtypeStruct((T, D), BF16),
        compiler_params=_params("parallel", "parallel"),
        name="conv_gate",
    )(u, u, b, conv_w)


def short_conv_layer(h, gain, w_in, conv_w, w_out):
    b, u = conv_in_proj(h, gain, w_in)
    y = conv_gate(u, b, conv_w)
    return matmul_residual(y, w_out, h)


HG_ROWS = 512
HG_SUB = 8
HG_LEVELS = 6


def _hgrn_level_table(n):
    tb = np.arange(n)[:, None] // HG_SUB
    sb = np.arange(n)[None, :] // HG_SUB
    x = np.maximum(tb ^ sb, 1)
    lvl = np.floor(np.log2(x)).astype(np.int32)
    return np.where(tb > sb, lvl, -1).astype(np.int32)


def _dot_nt(a, b):
    return lax.dot_general(a, b, (((1,), (1,)), ((), ())), preferred_element_type=F32)


def _hgrn_kernel(q_ref, f_ref, i_ref, g_ref, lb_ref, gn_ref, lvl_ref, o_ref, st_ref, qs, ks, bs, vs, os_):
    R = HG_ROWS

    @pl.when(pl.program_id(1) == 0)
    def _():
        st_ref[...] = jnp.zeros(st_ref.shape, F32)

    lb = lb_ref[...]
    qr = q_ref[...]
    q = qr * _sigmoid(qr)
    forget = lb + (1.0 - lb) * _sigmoid(f_ref[...])
    k = 1.0 - forget
    v = i_ref[...]
    row = lax.broadcasted_iota(I32, (R, LANES), 0)
    b = jnp.log(forget)
    step = 1
    while step < R:
        b = b + jnp.where(row >= step, pltpu.roll(b, step, axis=0), 0.0)
        step *= 2

    st = st_ref[...]
    o = _dot_nt((q * jnp.exp(b)).astype(BF16), st.astype(BF16))

    lvl = lvl_ref[...]
    scores = jnp.zeros((R, R), F32)
    for p in range(HG_LEVELS):
        half = HG_SUB << p
        blk = 2 * half
        bref = jnp.concatenate(
            [jnp.broadcast_to(b[j * blk + half - 1:j * blk + half], (blk, LANES)) for j in range(R // blk)], axis=0)
        qt = (q * jnp.exp(jnp.minimum(b - bref, 0.0))).astype(BF16)
        kt = (k * jnp.exp(jnp.minimum(bref - b, 0.0))).astype(BF16)
        scores = scores + jnp.where(lvl == p, _dot_nt(qt, kt), 0.0)
    o = o + jnp.dot(scores.astype(BF16), v.astype(BF16), preferred_element_type=F32)

    qs[...] = q
    ks[...] = k
    bs[...] = b
    vs[...] = v
    os_[...] = o
    row8 = lax.broadcasted_iota(I32, (HG_SUB, LANES), 0)

    def group(gi, carry):
        r0 = pl.multiple_of(gi * HG_SUB, HG_SUB)
        qb = qs[pl.ds(r0, HG_SUB), :]
        kb = ks[pl.ds(r0, HG_SUB), :]
        bb = bs[pl.ds(r0, HG_SUB), :]
        vb = vs[pl.ds(r0, HG_SUB), :]
        acc = jnp.zeros((HG_SUB, LANES), F32)
        for s in range(HG_SUB):
            decay = jnp.exp(jnp.where(row8 >= s, bb - bb[s:s + 1], -jnp.inf))
            a = jnp.sum(qb * kb[s:s + 1] * decay, axis=-1, keepdims=True)
            acc = acc + a * vb[s:s + 1]
        os_[pl.ds(r0, HG_SUB), :] += acc
        return carry

    lax.fori_loop(0, R // HG_SUB, group, 0, unroll=8)

    b_last = b[R - 1:R]
    kd = (k * jnp.exp(b_last - b)).astype(BF16)
    st_ref[...] = st * jnp.exp(b_last) + jnp.dot(v.T.astype(BF16), kd, preferred_element_type=F32)

    o = os_[...]
    o = o * lax.rsqrt(jnp.mean(o * o, axis=-1, keepdims=True) + RMS_EPS) * gn_ref[...]
    o_ref[...] = (o * _sigmoid(g_ref[...])).astype(o_ref.dtype)


def hgrn2_core(z, lb, gnorm):
    T = z.shape[0]
    D = z.shape[1] // 4
    R = HG_ROWS
    nh = D // LANES
    lvl = jnp.asarray(_hgrn_level_table(R))
    blk = lambda off: pl.BlockSpec((R, LANES), lambda hd, r: (r, off * nh + hd))
    return pl.pallas_call(
        _hgrn_kernel,
        grid=(nh, T // R),
        in_specs=[
            blk(0), blk(1), blk(2), blk(3),
            pl.BlockSpec((1, LANES), lambda hd, r: (0, hd)),
            pl.BlockSpec((1, LANES), lambda hd, r: (0, 0)),
            pl.BlockSpec((R, R), lambda hd, r: (0, 0)),
        ],
        out_specs=pl.BlockSpec((R, LANES), lambda hd, r: (r, hd)),
        out_shape=jax.ShapeDtypeStruct((T, D), BF16),
        scratch_shapes=[pltpu.VMEM((LANES, LANES), F32)] + [pltpu.VMEM((R, LANES), F32)] * 5,
        compiler_params=_params("parallel", "arbitrary"),
        name="hgrn2_core",
    )(z, z, z, z, lb.reshape(1, D), gnorm.reshape(1, LANES), lvl)


def hgrn2_layer(h, gain, w_in, gnorm, w_out, lb):
    z = norm_proj(h, gain, w_in)
    og = hgrn2_core(z, lb, gnorm)
    return matmul_residual(og, w_out, h)


DSA_QB = 128
DSA_KC = 512
DSA_KB = DSA_KC // LANES
INT32_MIN = -(2 ** 31)


def _rope_tables(T, rot_dim):
    half = rot_dim // 2
    inv_freq = 1.0 / (ROPE_THETA ** (jnp.arange(0, rot_dim, 2, dtype=F32) / rot_dim))
    ang = jnp.arange(T, dtype=F32)[:, None] * inv_freq[None, :]
    cos, sin = jnp.cos(ang), jnp.sin(ang)
    ones = jnp.ones((T, HEAD_DIM - rot_dim), F32)
    zeros_h = jnp.zeros((T, half), F32)
    zeros_r = jnp.zeros((T, HEAD_DIM - rot_dim), F32)
    c = jnp.concatenate([cos, cos, ones], axis=1)
    s_lo = jnp.concatenate([-sin, zeros_h, zeros_r], axis=1)
    s_hi = jnp.concatenate([zeros_h, sin, zeros_r], axis=1)
    return c, s_lo, s_hi


def _rope_proj_kernel(h_ref, g_ref, w_ref, c_ref, slo_ref, shi_ref, o_ref, xn_ref, *, half, scale, n_scaled):
    @pl.when(pl.program_id(1) == 0)
    def _():
        xn_ref[...] = _norm_rows(h_ref[...], g_ref[...]).astype(BF16)

    y = jnp.dot(xn_ref[...], w_ref[...], preferred_element_type=F32)
    y = y * jnp.where(pl.program_id(1) < n_scaled, scale, 1.0)
    c, slo, shi = c_ref[...], slo_ref[...], shi_ref[...]
    n_rb, hb = o_ref.shape[0], o_ref.shape[1]
    for hh in range(hb):
        yh = y[:, hh * LANES:(hh + 1) * LANES]
        if 2 * half == LANES:
            r = yh * c + pltpu.roll(yh, half, axis=1) * (slo + shi)
        else:
            r = yh * c + pltpu.roll(yh, LANES - half, axis=1) * slo + pltpu.roll(yh, half, axis=1) * shi
        r = r.astype(o_ref.dtype)
        for rb in range(n_rb):
            o_ref[rb, hh] = r[rb * LANES:(rb + 1) * LANES]


def rope_proj(h, gain, w, tables, *, rot_dim, scale=1.0, n_scaled=0, tm=1024, hb=4):
    T, D = h.shape
    n_heads = w.shape[1] // LANES
    tm, hb = min(tm, T), min(hb, n_heads)
    tn = hb * LANES
    tbl = pl.BlockSpec((tm, LANES), lambda i, j: (i, 0))
    return pl.pallas_call(
        functools.partial(_rope_proj_kernel, half=rot_dim // 2, scale=scale, n_scaled=n_scaled),
        grid=(T // tm, n_heads // hb),
        in_specs=[
            pl.BlockSpec((tm, D), lambda i, j: (i, 0)),
            pl.BlockSpec((1, D), lambda i, j: (0, 0)),
            pl.BlockSpec((D, tn), lambda i, j: (0, j)),
            tbl, tbl, tbl,
        ],
        out_specs=pl.BlockSpec((tm // LANES, hb, LANES, LANES), lambda i, j: (i, j, 0, 0)),
        out_shape=jax.ShapeDtypeStruct((T // LANES, n_heads, LANES, LANES), BF16),
        scratch_shapes=[pltpu.VMEM((tm, D), BF16)],
        compiler_params=_params("parallel", "arbitrary"),
        name="rope_proj",
    )(h, gain.reshape(1, D), w, *tables)


def _f32_from_ordered_bits(key):
    return pltpu.bitcast(key ^ ((key >> 31) & 0x7FFFFFFF), F32)


def _dsa_kernel(q_ref, iq_ref, iw_ref, k_ref, v_ref, ik_ref, o_ref, score_ref, ot_ref, m_ref, s_ref, *, n_sel):
    i = pl.program_id(0)
    n_chunks = (i * DSA_QB + DSA_QB + DSA_KC - 1) // DSA_KC
    q_pos = i * DSA_QB + lax.broadcasted_iota(I32, (DSA_KC, LANES), 1)
    row_iota = lax.broadcasted_iota(I32, (DSA_KC, LANES), 0)

    iw_t = iw_ref[...].T
    iq2 = iq_ref[0].reshape(IDX_HEADS * DSA_QB, HEAD_DIM)

    def index_chunk(c, carry):
        ikc = ik_ref[pl.ds(c * DSA_KB, DSA_KB), 0].reshape(DSA_KC, HEAD_DIM)
        d = _dot_nt(ikc, iq2)
        acc = jnp.zeros((DSA_KC, LANES), F32)
        for hh in range(IDX_HEADS):
            acc = acc + iw_t[hh:hh + 1, :] * jnp.maximum(d[:, hh * LANES:(hh + 1) * LANES], 0.0)
        causal = c * DSA_KC + row_iota <= q_pos
        score_ref[pl.ds(pl.multiple_of(c * DSA_KC, DSA_KC), DSA_KC), :] = jnp.where(causal, acc, -jnp.inf)
        return carry

    lax.fori_loop(0, n_chunks, index_chunk, 0)

    def count_ge(cand):
        def body(c, acc):
            sc = score_ref[pl.ds(pl.multiple_of(c * DSA_KC, DSA_KC), DSA_KC), :]
            hit = jnp.where(sc >= cand, 1, 0).astype(I32)
            return acc + jnp.sum(hit.reshape(DSA_KC // SUBLANES, SUBLANES, LANES), axis=0)

        acc = lax.fori_loop(0, n_chunks, body, jnp.zeros((SUBLANES, LANES), I32))
        return jnp.sum(acc, axis=0, keepdims=True)

    thr_key = jnp.where(count_ge(jnp.zeros((1, LANES), F32)) >= n_sel, 0, INT32_MIN).astype(I32)

    def bit_step(bi, thr_key):
        cand = thr_key + jnp.left_shift(jnp.int32(1), 30 - bi)
        return jnp.where(count_ge(_f32_from_ordered_bits(cand)) >= n_sel, cand, thr_key)

    thr_key = lax.fori_loop(0, 31, bit_step, thr_key)
    thr = jnp.where(thr_key == INT32_MIN, -jnp.inf, _f32_from_ordered_bits(thr_key))

    m_ref[...] = jnp.full(m_ref.shape, -1e29, F32)
    s_ref[...] = jnp.zeros(s_ref.shape, F32)
    ot_ref[...] = jnp.zeros(ot_ref.shape, F32)

    def attend_chunk(c, carry):
        r0 = pl.multiple_of(c * DSA_KC, DSA_KC)
        sc = score_ref[pl.ds(r0, DSA_KC), :]
        keep = jnp.logical_and(sc >= thr, c * DSA_KC + row_iota <= q_pos)
        bias = jnp.where(keep, 0.0, NEG_BIG)
        bias4 = jnp.concatenate([bias] * ATT_GROUP, axis=1)
        for g in range(ATT_KV_HEADS):
            kc = k_ref[pl.ds(c * DSA_KB, DSA_KB), g].reshape(DSA_KC, HEAD_DIM)
            vc = v_ref[pl.ds(c * DSA_KB, DSA_KB), g].reshape(DSA_KC, HEAD_DIM)
            qg = q_ref[0, g * ATT_GROUP:(g + 1) * ATT_GROUP].reshape(ATT_GROUP * DSA_QB, HEAD_DIM)
            logit = _dot_nt(kc, qg) + bias4
            m_old = m_ref[g]
            m_new = jnp.maximum(m_old, jnp.max(logit, axis=0, keepdims=True))
            alpha = jnp.exp(m_old - m_new)
            p = jnp.exp(logit - m_new)
            s_ref[g] = alpha * s_ref[g] + jnp.sum(p, axis=0, keepdims=True)
            pv = lax.dot_general(vc, p.astype(BF16), (((0,), (0,)), ((), ())), preferred_element_type=F32)
            ot_ref[g] = ot_ref[g] * alpha + pv
            m_ref[g] = m_new
        return carry

    lax.fori_loop(0, n_chunks, attend_chunk, 0)

    for g in range(ATT_KV_HEADS):
        o_t = ot_ref[g] * (1.0 / s_ref[g])
        for hh in range(ATT_GROUP):
            head = g * ATT_GROUP + hh
            o_ref[:, head * HEAD_DIM:(head + 1) * HEAD_DIM] = o_t[:, hh * LANES:(hh + 1) * LANES].T.astype(o_ref.dtype)


def dsa_core(qk, v, iq, ik, iw, n_sel):
    nb = qk.shape[0]
    T = nb * LANES
    resident = lambda heads, first: pl.BlockSpec((nb, heads, LANES, LANES), lambda i: (0, first // heads, 0, 0),
                                                 pipeline_mode=pl.Buffered(1))
    return pl.pallas_call(
        functools.partial(_dsa_kernel, n_sel=n_sel),
        grid=(nb,),
        in_specs=[
            pl.BlockSpec((1, ATT_HEADS, LANES, LANES), lambda i: (i, 0, 0, 0)),
            pl.BlockSpec((1, IDX_HEADS, LANES, LANES), lambda i: (i, 0, 0, 0)),
            pl.BlockSpec((DSA_QB, LANES), lambda i: (i, 0)),
            resident(ATT_KV_HEADS, ATT_HEADS),
            resident(ATT_KV_HEADS, 0),
            resident(1, 0),
        ],
        out_specs=pl.BlockSpec((DSA_QB, ATT_HEADS * HEAD_DIM), lambda i: (i, 0)),
        out_shape=jax.ShapeDtypeStruct((T, ATT_HEADS * HEAD_DIM), BF16),
        scratch_shapes=[
            pltpu.VMEM((T, LANES), F32),
            pltpu.VMEM((ATT_KV_HEADS, HEAD_DIM, ATT_GROUP * DSA_QB), F32),
            pltpu.VMEM((ATT_KV_HEADS, 1, ATT_GROUP * DSA_QB), F32),
            pltpu.VMEM((ATT_KV_HEADS, 1, ATT_GROUP * DSA_QB), F32),
        ],
        compiler_params=_params("parallel"),
        name="dsa_core",
    )(qk, iq, iw, qk, v, ik)


def dsa_layer(h, gain, w_in, w_out):
    T, D = h.shape
    nq, nkv = ATT_HEADS * HEAD_DIM, ATT_KV_HEADS * HEAD_DIM
    ni = IDX_HEADS * HEAD_DIM
    w_qk = w_in[:, :nq + nkv].astype(BF16)
    w_v = w_in[:, nq + nkv:nq + 2 * nkv].astype(BF16)
    o_i = nq + 2 * nkv
    w_iq = w_in[:, o_i:o_i + ni].astype(BF16)
    w_ik = w_in[:, o_i + ni:o_i + ni + HEAD_DIM].astype(BF16)
    w_iw = jnp.zeros((D, LANES), F32).at[:, :IDX_HEADS].set(w_in[:, o_i + ni + HEAD_DIM:]).astype(BF16)
    qk = rope_proj(h, gain, w_qk, _rope_tables(T, HEAD_DIM), rot_dim=HEAD_DIM, scale=HEAD_DIM ** -0.5,
                   n_scaled=ATT_HEADS // 4)
    idx_tables = _rope_tables(T, IDX_ROPE_DIM)
    iq = rope_proj(h, gain, w_iq, idx_tables, rot_dim=IDX_ROPE_DIM)
    ik = rope_proj(h, gain, w_ik, idx_tables, rot_dim=IDX_ROPE_DIM)
    v = norm_proj(h, gain, w_v, out_dtype=BF16).reshape(T // LANES, LANES, ATT_KV_HEADS, HEAD_DIM).transpose(0, 2, 1, 3)
    iw = norm_proj(h, gain, w_iw) * (IDX_HEADS ** -0.5 * HEAD_DIM ** -0.5)
    o = dsa_core(qk, v, iq, ik, iw, min(TOPK_MAX, T // 4))
    return matmul_residual(o, w_out, h)


MOE_TILE = 256
MOE_EXPERT_VMEM_LIMIT_BYTES = 62 * 1024 * 1024


SLAB_LINES = D_MODEL // LANES
SLAB_PITCH = 20


def _store_slabs(ref, x):
    n = x.shape[0]
    for c in range(SLAB_LINES):
        ref[pl.ds(c, n, stride=SLAB_PITCH), :] = x[:, c * LANES:(c + 1) * LANES]
    for c in range(SLAB_LINES, SLAB_PITCH):
        ref[pl.ds(c, n, stride=SLAB_PITCH), :] = jnp.zeros((n, LANES), x.dtype)


def _load_slabs(ref, first_slab, n):
    return jnp.concatenate(
        [ref[pl.ds(first_slab * SLAB_PITCH + c, n, stride=SLAB_PITCH), :] for c in range(SLAB_LINES)], axis=1)


def _router_kernel(h_ref, g_ref, rw_ref, rb_ref, xn_ref, idx_ref, gate_ref):
    xn = _norm_rows(h_ref[...], g_ref[...])
    _store_slabs(xn_ref, xn)
    logits = jnp.dot(xn, rw_ref[...], preferred_element_type=F32, precision=lax.Precision.HIGHEST) + rb_ref[...]
    lane = lax.broadcasted_iota(I32, logits.shape, 1)
    vals, idxs = [], []
    for _ in range(MOE_TOP_K):
        m = jnp.max(logits, axis=-1, keepdims=True)
        sel = jnp.min(jnp.where(logits == m, lane, LANES), axis=-1, keepdims=True)
        vals.append(m)
        idxs.append(sel)
        logits = jnp.where(lane == sel, -jnp.inf, logits)
    ex = [jnp.exp(v - vals[0]) for v in vals]
    den = ex[0] + ex[1] + ex[2] + ex[3]
    idx_out = jnp.zeros(lane.shape, I32)
    gate_out = jnp.zeros(lane.shape, F32)
    for k in range(MOE_TOP_K):
        idx_out = jnp.where(lane == k, idxs[k], idx_out)
        gate_out = jnp.where(lane == k, ex[k] / den, gate_out)
    idx_ref[...] = idx_out
    gate_ref[...] = gate_out


def moe_router(h, gain, router_w, router_b, *, tm=256):
    T, D = h.shape
    rw = jnp.zeros((D, LANES), F32).at[:, :N_EXPERTS].set(router_w)
    rb = jnp.full((1, LANES), -jnp.inf, F32).at[0, :N_EXPERTS].set(router_b)
    return pl.pallas_call(
        _router_kernel,
        grid=(T // tm,),
        in_specs=[
            pl.BlockSpec((tm, D), lambda i: (i, 0)),
            pl.BlockSpec((1, D), lambda i: (0, 0)),
            pl.BlockSpec((D, LANES), lambda i: (0, 0)),
            pl.BlockSpec((1, LANES), lambda i: (0, 0)),
        ],
        out_specs=[
            pl.BlockSpec((tm * SLAB_PITCH, LANES), lambda i: (i, 0)),
            pl.BlockSpec((tm, LANES), lambda i: (i, 0)),
            pl.BlockSpec((tm, LANES), lambda i: (i, 0)),
        ],
        out_shape=[
            jax.ShapeDtypeStruct((T * SLAB_PITCH, LANES), F32),
            jax.ShapeDtypeStruct((T, LANES), I32),
            jax.ShapeDtypeStruct((T, LANES), F32),
        ],
        compiler_params=_params("parallel"),
        name="moe_router",
    )(h, gain.reshape(1, D), rw, rb)


def _route(idx, n_slots, tile):
    T = idx.shape[0]
    MOE_TILE = tile
    n_tiles = n_slots // MOE_TILE
    flat_e = idx.reshape(-1)
    onehot = (flat_e[:, None] == jnp.arange(N_EXPERTS, dtype=I32)[None, :]).astype(I32)
    csum = jnp.cumsum(onehot, axis=0)
    rank = jnp.sum(csum * onehot, axis=1) - 1
    counts = csum[-1]
    padded = ((counts + MOE_TILE - 1) // MOE_TILE) * MOE_TILE
    ends = jnp.cumsum(padded)
    starts = ends - padded
    slot = jnp.sum(starts[None, :] * onehot, axis=1) + rank
    token = jnp.arange(T * MOE_TOP_K, dtype=I32) // MOE_TOP_K
    tok_of_slot = jnp.zeros((n_slots,), I32).at[slot].set(token, unique_indices=True)
    tile_start = jnp.arange(n_tiles, dtype=I32) * MOE_TILE
    tile_expert = jnp.sum((tile_start[:, None] >= ends[None, :]).astype(I32), axis=1)
    tile_expert = jnp.minimum(tile_expert, N_EXPERTS - 1)
    n_used = ends[-1] // MOE_TILE
    return tok_of_slot, slot.reshape(T, MOE_TOP_K), tile_expert, n_used.reshape(1)


def _start_slab_gather(ids_ref, n_rows, src_hbm, dst_ref, sem, *, both_queues=False):
    for r in range(n_rows):
        line = ids_ref[0, 0, r]
        copy = pltpu.make_async_copy(src_hbm.at[pl.ds(line, SLAB_LINES), :],
                                     dst_ref.at[pl.ds(r * SLAB_PITCH, SLAB_LINES), :], sem)
        copy.start(priority=r % 2 if both_queues else 0)


def _wait_slab_gather(n_rows, src_hbm, dst_ref, sem):
    n = n_rows * SLAB_LINES
    pltpu.make_async_copy(src_hbm.at[pl.ds(0, n), :], dst_ref.at[pl.ds(0, n), :], sem).wait()


def _expert_kernel(texp_ref, nused_ref, ids_cur_ref, ids_nxt_ref, x_hbm, w1_ref, b1_ref, w2_ref, b2_ref,
                   y_ref, xbuf, sem, *, tile, both_queues):
    t = pl.program_id(0)
    n_used = nused_ref[0]
    slot = t % 2

    @pl.when(jnp.logical_and(t == 0, n_used > 0))
    def _():
        _start_slab_gather(ids_cur_ref, tile, x_hbm, xbuf.at[0], sem.at[0], both_queues=both_queues)

    @pl.when(t + 1 < n_used)
    def _():
        _start_slab_gather(ids_nxt_ref, tile, x_hbm, xbuf.at[1 - slot], sem.at[1 - slot], both_queues=both_queues)

    @pl.when(t < n_used)
    def _():
        _wait_slab_gather(tile, x_hbm, xbuf.at[slot], sem.at[slot])
        x = _load_slabs(xbuf.at[slot], 0, tile).astype(BF16)
        hcat = jnp.dot(x, w1_ref[0].astype(BF16), preferred_element_type=F32) + b1_ref[0]
        gate = jnp.minimum(hcat[:, :D_EXPERT], SWIGLU_LIMIT)
        up = jnp.clip(hcat[:, D_EXPERT:], -SWIGLU_LIMIT, SWIGLU_LIMIT)
        glu = gate * _sigmoid(SWIGLU_ALPHA * gate)
        act = ((up + 1.0) * glu).astype(BF16)
        _store_slabs(y_ref, jnp.dot(act, w2_ref[0].astype(BF16), preferred_element_type=F32) + b2_ref[0])

    @pl.when(t >= n_used)
    def _():
        y_ref[...] = jnp.zeros(y_ref.shape, F32)


def moe_experts(xn_slabs, tok_of_slot, tile_expert, n_used, w1, b1, w2, b2, tile, both_queues):
    D = D_MODEL
    n_slots = tok_of_slot.shape[0]
    n_tiles = n_slots // tile
    ids = (tok_of_slot * SLAB_PITCH).reshape(n_tiles, 1, tile)
    grid_spec = pltpu.PrefetchScalarGridSpec(
        num_scalar_prefetch=2,
        grid=(n_tiles,),
        in_specs=[
            pl.BlockSpec((1, 1, tile), lambda t, te, nu: (t, 0, 0), memory_space=pltpu.SMEM),
            pl.BlockSpec((1, 1, tile), lambda t, te, nu: (jnp.minimum(t + 1, n_tiles - 1), 0, 0),
                         memory_space=pltpu.SMEM),
            pl.BlockSpec(memory_space=pl.ANY),
            pl.BlockSpec((1, D, 2 * D_EXPERT), lambda t, te, nu: (te[t], 0, 0)),
            pl.BlockSpec((1, 1, 2 * D_EXPERT), lambda t, te, nu: (te[t], 0, 0)),
            pl.BlockSpec((1, D_EXPERT, D), lambda t, te, nu: (te[t], 0, 0)),
            pl.BlockSpec((1, 1, D), lambda t, te, nu: (te[t], 0, 0)),
        ],
        out_specs=pl.BlockSpec((tile * SLAB_PITCH, LANES), lambda t, te, nu: (t, 0)),
        scratch_shapes=[pltpu.VMEM((2, tile * SLAB_PITCH, LANES), F32), pltpu.SemaphoreType.DMA((2,))],
    )
    return pl.pallas_call(
        functools.partial(_expert_kernel, tile=tile, both_queues=both_queues),
        grid_spec=grid_spec,
        out_shape=jax.ShapeDtypeStruct((n_slots * SLAB_PITCH, LANES), F32),
        compiler_params=pltpu.CompilerParams(dimension_semantics=("arbitrary",),
                                             vmem_limit_bytes=MOE_EXPERT_VMEM_LIMIT_BYTES),
        name="moe_experts",
    )(tile_expert, n_used, ids, ids, xn_slabs, w1, b1.reshape(N_EXPERTS, 1, -1), w2, b2.reshape(N_EXPERTS, 1, -1))


COMBINE_TOKENS = 128


def _combine_kernel(ids_cur_ref, ids_nxt_ref, y_hbm, gate_ref, h_ref, o_ref, ybuf, sem):
    t = pl.program_id(0)
    n = pl.num_programs(0)
    slot = t % 2
    rows = COMBINE_TOKENS * MOE_TOP_K

    @pl.when(t == 0)
    def _():
        _start_slab_gather(ids_cur_ref, rows, y_hbm, ybuf.at[0], sem.at[0], both_queues=True)

    @pl.when(t + 1 < n)
    def _():
        _start_slab_gather(ids_nxt_ref, rows, y_hbm, ybuf.at[1 - slot], sem.at[1 - slot], both_queues=True)

    _wait_slab_gather(rows, y_hbm, ybuf.at[slot], sem.at[slot])
    gates = gate_ref[...]
    acc = h_ref[...]
    for k in range(MOE_TOP_K):
        acc = acc + gates[:, k:k + 1] * _load_slabs(ybuf.at[slot], k * COMBINE_TOKENS, COMBINE_TOKENS)
    o_ref[...] = acc


def moe_combine(y_slabs, slot_of, gate_pad, h):
    T, D = h.shape
    n_steps = T // COMBINE_TOKENS
    rows = COMBINE_TOKENS * MOE_TOP_K
    ids = (slot_of * SLAB_PITCH).reshape(n_steps, COMBINE_TOKENS, MOE_TOP_K).transpose(0, 2, 1)
    ids = ids.reshape(n_steps, 1, rows)
    return pl.pallas_call(
        _combine_kernel,
        grid=(n_steps,),
        in_specs=[
            pl.BlockSpec((1, 1, rows), lambda t: (t, 0, 0), memory_space=pltpu.SMEM),
            pl.BlockSpec((1, 1, rows), lambda t: (jnp.minimum(t + 1, n_steps - 1), 0, 0), memory_space=pltpu.SMEM),
            pl.BlockSpec(memory_space=pl.ANY),
            pl.BlockSpec((COMBINE_TOKENS, LANES), lambda t: (t, 0)),
            pl.BlockSpec((COMBINE_TOKENS, D), lambda t: (t, 0)),
        ],
        out_specs=pl.BlockSpec((COMBINE_TOKENS, D), lambda t: (t, 0)),
        out_shape=jax.ShapeDtypeStruct((T, D), F32),
        scratch_shapes=[pltpu.VMEM((2, rows * SLAB_PITCH, LANES), F32), pltpu.SemaphoreType.DMA((2,))],
        compiler_params=_params("arbitrary"),
        name="moe_combine",
    )(ids, ids, y_slabs, gate_pad, h)


def moe_layer(h, gain, router_w, router_b, w1, b1, w2, b2, tile=MOE_TILE, both_queues=False):
    T = h.shape[0]
    n_slots = T * MOE_TOP_K + N_EXPERTS * tile
    xn_slabs, idx_pad, gate_pad = moe_router(h, gain, router_w, router_b)
    tok_of_slot, slot_of, tile_expert, n_used = _route(idx_pad[:, :MOE_TOP_K], n_slots, tile)
    y_slabs = moe_experts(xn_slabs, tok_of_slot, tile_expert, n_used, w1, b1, w2, b2, tile, both_queues)
    return moe_combine(y_slabs, slot_of, gate_pad, h)


def kernel(x, hgrn_lb_logits,
           l0_norm_mix, l0_conv_in, l0_conv_w, l0_conv_out, l0_norm_ffn,
           l0_router_w, l0_router_b, l0_exp_w1, l0_exp_b1, l0_exp_w2, l0_exp_b2,
           l1_norm_mix, l1_hgrn_in, l1_hgrn_gnorm, l1_hgrn_out, l1_norm_ffn,
           l1_router_w, l1_router_b, l1_exp_w1, l1_exp_b1, l1_exp_w2, l1_exp_b2,
           l2_norm_mix, l2_dsa_in, l2_dsa_out, l2_norm_ffn,
           l2_router_w, l2_router_b, l2_exp_w1, l2_exp_b1, l2_exp_w2, l2_exp_b2,
           l3_norm_mix, l3_conv_in, l3_conv_w, l3_conv_out, l3_norm_ffn,
           l3_router_w, l3_router_b, l3_exp_w1, l3_exp_b1, l3_exp_w2, l3_exp_b2,
           final_norm_gain):
    B, T, D = x.shape
    bf = lambda w: w.astype(BF16)
    h = x.reshape(B * T, D)

    h = short_conv_layer(h, l0_norm_mix, bf(l0_conv_in), l0_conv_w, bf(l0_conv_out))
    h = moe_layer(h, l0_norm_ffn, l0_router_w, l0_router_b, l0_exp_w1, l0_exp_b1, l0_exp_w2, l0_exp_b2)

    lb_soft = jax.nn.softmax(hgrn_lb_logits.astype(F32), axis=0)
    lb_all = jnp.cumsum(lb_soft, axis=0) - lb_soft[0:1]
    h = hgrn2_layer(h, l1_norm_mix, bf(l1_hgrn_in), l1_hgrn_gnorm, bf(l1_hgrn_out), lb_all[1])
    h = moe_layer(h, l1_norm_ffn, l1_router_w, l1_router_b, l1_exp_w1, l1_exp_b1, l1_exp_w2, l1_exp_b2, both_queues=True)

    h = dsa_layer(h, l2_norm_mix, l2_dsa_in, bf(l2_dsa_out))
    h = moe_layer(h, l2_norm_ffn, l2_router_w, l2_router_b, l2_exp_w1, l2_exp_b1, l2_exp_w2, l2_exp_b2)

    h = short_conv_layer(h, l3_norm_mix, bf(l3_conv_in), l3_conv_w, bf(l3_conv_out))
    h = moe_layer(h, l3_norm_ffn, l3_router_w, l3_router_b, l3_exp_w1, l3_exp_b1, l3_exp_w2, l3_exp_b2, both_queues=True)

    return final_norm(h, final_norm_gain).reshape(B, T, D)
```

```python
import functools

import numpy as np
import jax
import jax.numpy as jnp
from jax import lax
from jax.experimental import pallas as pl
from jax.experimental.pallas import tpu as pltpu

F32 = jnp.float32
BF16 = jnp.bfloat16
I32 = jnp.int32

D_MODEL = 2048
RMS_EPS = 1e-6
ROPE_THETA = 10000.0

HG_HEADS = 16
HG_DK = 128

ATT_HEADS = 16
ATT_KV_HEADS = 4
ATT_GROUP = ATT_HEADS // ATT_KV_HEADS
HEAD_DIM = 128
IDX_HEADS = 16
IDX_ROPE_DIM = 64
TOPK_MAX = 256

N_EXPERTS = 32
MOE_TOP_K = 4
D_EXPERT = D_MODEL // 2
SWIGLU_LIMIT = 7.0
SWIGLU_ALPHA = 1.702

LANES = 128
SUBLANES = 8
VMEM_LIMIT_BYTES = 56 * 1024 * 1024
NEG_BIG = -1e30


def _params(*sem):
    return pltpu.CompilerParams(dimension_semantics=sem, vmem_limit_bytes=VMEM_LIMIT_BYTES)


def _norm_rows(x, g):
    ms = jnp.mean(x * x, axis=-1, keepdims=True)
    return x * lax.rsqrt(ms + RMS_EPS) * g


def _sigmoid(x):
    return 1.0 / (1.0 + jnp.exp(-x))


def _proj_kernel(h_ref, g_ref, w_ref, o_ref, xn_ref):
    @pl.when(pl.program_id(1) == 0)
    def _():
        xn_ref[...] = _norm_rows(h_ref[...], g_ref[...]).astype(BF16)

    o_ref[...] = jnp.dot(xn_ref[...], w_ref[...], preferred_element_type=F32).astype(o_ref.dtype)


def norm_proj(h, gain, w, *, tm=1024, tn=1024, out_dtype=F32):
    T, D = h.shape
    N = w.shape[1]
    tm, tn = min(tm, T), min(tn, N)
    return pl.pallas_call(
        _proj_kernel,
        grid=(T // tm, N // tn),
        in_specs=[
            pl.BlockSpec((tm, D), lambda i, j: (i, 0)),
            pl.BlockSpec((1, D), lambda i, j: (0, 0)),
            pl.BlockSpec((D, tn), lambda i, j: (0, j)),
        ],
        out_specs=pl.BlockSpec((tm, tn), lambda i, j: (i, j)),
        out_shape=jax.ShapeDtypeStruct((T, N), out_dtype),
        scratch_shapes=[pltpu.VMEM((tm, D), BF16)],
        compiler_params=_params("parallel", "arbitrary"),
        name="norm_proj",
    )(h, gain.reshape(1, D), w)


def _matmul_res_kernel(a_ref, w_ref, h_ref, o_ref):
    o_ref[...] = h_ref[...] + jnp.dot(a_ref[...], w_ref[...], preferred_element_type=F32)


def matmul_residual(a, w, h, *, tm=1024, tn=1024):
    T, K = a.shape
    N = w.shape[1]
    tm, tn = min(tm, T), min(tn, N)
    return pl.pallas_call(
        _matmul_res_kernel,
        grid=(T // tm, N // tn),
        in_specs=[
            pl.BlockSpec((tm, K), lambda i, j: (i, 0)),
            pl.BlockSpec((K, tn), lambda i, j: (0, j)),
            pl.BlockSpec((tm, tn), lambda i, j: (i, j)),
        ],
        out_specs=pl.BlockSpec((tm, tn), lambda i, j: (i, j)),
        out_shape=jax.ShapeDtypeStruct((T, N), F32),
        compiler_params=_params("parallel", "arbitrary"),
        name="matmul_residual",
    )(a, w, h)


def _conv_in_kernel(h_ref, g_ref, wb_ref, wc_ref, wh_ref, b_ref, u_ref, xn_ref):
    @pl.when(pl.program_id(1) == 0)
    def _():
        xn_ref[...] = _norm_rows(h_ref[...], g_ref[...]).astype(BF16)

    xn = xn_ref[...]
    b_ref[...] = jnp.dot(xn, wb_ref[...], preferred_element_type=F32)
    c = jnp.dot(xn, wc_ref[...], preferred_element_type=F32)
    hh = jnp.dot(xn, wh_ref[...], preferred_element_type=F32)
    u_ref[...] = c * hh


def conv_in_proj(h, gain, w_in, *, tm=1024, tn=512):
    T, D = h.shape
    tm = min(tm, T)
    nb = D // tn
    out = jax.ShapeDtypeStruct((T, D), F32)
    return pl.pallas_call(
        _conv_in_kernel,
        grid=(T // tm, nb),
        in_specs=[
            pl.BlockSpec((tm, D), lambda i, j: (i, 0)),
            pl.BlockSpec((1, D), lambda i, j: (0, 0)),
            pl.BlockSpec((D, tn), lambda i, j: (0, j)),
            pl.BlockSpec((D, tn), lambda i, j: (0, j + nb)),
            pl.BlockSpec((D, tn), lambda i, j: (0, j + 2 * nb)),
        ],
        out_specs=[pl.BlockSpec((tm, tn), lambda i, j: (i, j))] * 2,
        out_shape=[out, out],
        scratch_shapes=[pltpu.VMEM((tm, D), BF16)],
        compiler_params=_params("parallel", "arbitrary"),
        name="conv_in_proj",
    )(h, gain.reshape(1, D), w_in, w_in, w_in)


def _shift_rows(u, prev, s):
    r = pltpu.roll(u, s, axis=0)
    p = pltpu.roll(prev, s, axis=0)
    row = lax.broadcasted_iota(I32, p.shape, 0)
    head = jnp.where(row < s, p, r[:SUBLANES])
    return jnp.concatenate([head, r[SUBLANES:]], axis=0)


def _conv_gate_kernel(u_ref, up_ref, b_ref, cw_ref, y_ref):
    u = u_ref[...]
    prev = jnp.where(pl.program_id(0) > 0, up_ref[...], 0.0)
    cw = cw_ref[...]
    conv = cw[2:3] * u + cw[1:2] * _shift_rows(u, prev, 1) + cw[0:1] * _shift_rows(u, prev, 2)
    y_ref[...] = (b_ref[...] * conv).astype(y_ref.dtype)


def conv_gate(u, b, conv_w, *, tm=512, tn=1024):
    T, D = u.shape
    rb = tm // SUBLANES
    return pl.pallas_call(
        _conv_gate_kernel,
        grid=(T // tm, D // tn),
        in_specs=[
            pl.BlockSpec((tm, tn), lambda i, j: (i, j)),
            pl.BlockSpec((SUBLANES, tn), lambda i, j: (jnp.maximum(i * rb - 1, 0), j)),
            pl.BlockSpec((tm, tn), lambda i, j: (i, j)),
            pl.BlockSpec((3, tn), lambda i, j: (0, j)),
        ],
        out_specs=pl.BlockSpec((tm, tn), lambda i, j: (i, j)),
        out_shape=jax.ShapeDtypeStruct((T, D), BF16),
        compiler_params=_params("parallel", "parallel"),
        name="conv_gate",
    )(u, u, b, conv_w)


def short_conv_layer(h, gain, w_in, conv_w, w_out):
    b, u = conv_in_proj(h, gain, w_in)
    y = conv_gate(u, b, conv_w)
    return matmul_residual(y, w_out, h)


HG_ROWS = 512
HG_SUB = 8
HG_LEVELS = 6
HG_DIAG = 128


def _hgrn_level_table(n):
    tb = np.arange(n)[:, None] // HG_SUB
    sb = np.arange(n)[None, :] // HG_SUB
    x = np.maximum(tb ^ sb, 1)
    lvl = np.floor(np.log2(x)).astype(np.int32)
    return np.where(tb > sb, lvl, -1).astype(np.int32)


def _dot_nt(a, b):
    return lax.dot_general(a, b, (((1,), (1,)), ((), ())), preferred_element_type=F32)


def _hgrn_kernel(q_ref, f_ref, i_ref, g_ref, lb_ref, gn_ref, lvl_ref, o_ref, st_ref, qs, ks, bs, vs, os_):
    R = HG_ROWS

    @pl.when(pl.program_id(1) == 0)
    def _():
        st_ref[...] = jnp.zeros(st_ref.shape, F32)

    lb = lb_ref[...]
    qr = q_ref[...]
    q = qr * _sigmoid(qr)
    forget = lb + (1.0 - lb) * _sigmoid(f_ref[...])
    k = 1.0 - forget
    v = i_ref[...]
    row = lax.broadcasted_iota(I32, (R, LANES), 0)
    b = jnp.log(forget)
    step = 1
    while step < R:
        b = b + jnp.where(row >= step, pltpu.roll(b, step, axis=0), 0.0)
        step *= 2

    st = st_ref[...]
    o = _dot_nt((q * jnp.exp(b)).astype(BF16), st.astype(BF16))

    lvl = lvl_ref[...]
    n_diag = R // HG_DIAG
    diag = [jnp.zeros((HG_DIAG, HG_DIAG), F32)] * n_diag
    below = {}
    for p in range(HG_LEVELS):
        half = HG_SUB << p
        blk = 2 * half
        bref = jnp.concatenate(
            [jnp.broadcast_to(b[j * blk + half - 1:j * blk + half], (blk, LANES)) for j in range(R // blk)], axis=0)
        qt = (q * jnp.exp(jnp.minimum(b - bref, 0.0))).astype(BF16)
        kt = (k * jnp.exp(jnp.minimum(bref - b, 0.0))).astype(BF16)
        if blk <= HG_DIAG:
            for d in range(n_diag):
                rows = slice(d * HG_DIAG, (d + 1) * HG_DIAG)
                diag[d] = diag[d] + jnp.where(lvl == p, _dot_nt(qt[rows], kt[rows]), 0.0)
        else:
            for j in range(R // blk):
                below[(blk, j)] = _dot_nt(qt[j * blk + half:(j + 1) * blk], kt[j * blk:j * blk + half])

    def score_block(first, size):
        if size == HG_DIAG:
            return diag[first // HG_DIAG]
        half = size // 2
        top = jnp.concatenate([score_block(first, half), jnp.zeros((half, half), F32)], axis=1)
        bottom = jnp.concatenate([below[(size, first // size)], score_block(first + half, half)], axis=1)
        return jnp.concatenate([top, bottom], axis=0)

    o = o + jnp.dot(score_block(0, R).astype(BF16), v.astype(BF16), preferred_element_type=F32)

    qs[...] = q
    ks[...] = k
    bs[...] = b
    vs[...] = v
    os_[...] = o
    row8 = lax.broadcasted_iota(I32, (HG_SUB, LANES), 0)

    def group(gi, carry):
        r0 = pl.multiple_of(gi * HG_SUB, HG_SUB)
        qb = qs[pl.ds(r0, HG_SUB), :]
        kb = ks[pl.ds(r0, HG_SUB), :]
        bb = bs[pl.ds(r0, HG_SUB), :]
        vb = vs[pl.ds(r0, HG_SUB), :]
        acc = jnp.zeros((HG_SUB, LANES), F32)
        for s in range(HG_SUB):
            decay = jnp.exp(jnp.where(row8 >= s, bb - bb[s:s + 1], -jnp.inf))
            a = jnp.sum(qb * kb[s:s + 1] * decay, axis=-1, keepdims=True)
            acc = acc + a * vb[s:s + 1]
        os_[pl.ds(r0, HG_SUB), :] += acc
        return carry

    lax.fori_loop(0, R // HG_SUB, group, 0, unroll=8)

    b_last = b[R - 1:R]
    kd = (k * jnp.exp(b_last - b)).astype(BF16)
    st_ref[...] = st * jnp.exp(b_last) + jnp.dot(v.T.astype(BF16), kd, preferred_element_type=F32)

    o = os_[...]
    o = o * lax.rsqrt(jnp.mean(o * o, axis=-1, keepdims=True) + RMS_EPS) * gn_ref[...]
    o_ref[...] = (o * _sigmoid(g_ref[...])).astype(o_ref.dtype)


def hgrn2_core(z, lb, gnorm):
    T = z.shape[0]
    D = z.shape[1] // 4
    R = HG_ROWS
    nh = D // LANES
    lvl = jnp.asarray(_hgrn_level_table(HG_DIAG))
    blk = lambda off: pl.BlockSpec((R, LANES), lambda hd, r: (r, off * nh + hd))
    return pl.pallas_call(
        _hgrn_kernel,
        grid=(nh, T // R),
        in_specs=[
            blk(0), blk(1), blk(2), blk(3),
            pl.BlockSpec((1, LANES), lambda hd, r: (0, hd)),
            pl.BlockSpec((1, LANES), lambda hd, r: (0, 0)),
            pl.BlockSpec((HG_DIAG, HG_DIAG), lambda hd, r: (0, 0)),
        ],
        out_specs=pl.BlockSpec((R, LANES), lambda hd, r: (r, hd)),
        out_shape=jax.ShapeDtypeStruct((T, D), BF16),
        scratch_shapes=[pltpu.VMEM((LANES, LANES), F32)] + [pltpu.VMEM((R, LANES), F32)] * 5,
        compiler_params=_params("parallel", "arbitrary"),
        name="hgrn2_core",
    )(z, z, z, z, lb.reshape(1, D), gnorm.reshape(1, LANES), lvl)


def hgrn2_layer(h, gain, w_in, gnorm, w_out, lb):
    z = norm_proj(h, gain, w_in)
    og = hgrn2_core(z, lb, gnorm)
    return matmul_residual(og, w_out, h)


DSA_QB = 128
DSA_KC = 512
DSA_KB = DSA_KC // LANES
INT32_MIN = -(2 ** 31)


def _rope_tables(T, rot_dim):
    half = rot_dim // 2
    inv_freq = 1.0 / (ROPE_THETA ** (jnp.arange(0, rot_dim, 2, dtype=F32) / rot_dim))
    ang = jnp.arange(T, dtype=F32)[:, None] * inv_freq[None, :]
    cos, sin = jnp.cos(ang), jnp.sin(ang)
    ones = jnp.ones((T, HEAD_DIM - rot_dim), F32)
    zeros_h = jnp.zeros((T, half), F32)
    zeros_r = jnp.zeros((T, HEAD_DIM - rot_dim), F32)
    c = jnp.concatenate([cos, cos, ones], axis=1)
    s_lo = jnp.concatenate([-sin, zeros_h, zeros_r], axis=1)
    s_hi = jnp.concatenate([zeros_h, sin, zeros_r], axis=1)
    return c, s_lo, s_hi


def _rope_proj_kernel(h_ref, g_ref, w_ref, c_ref, slo_ref, shi_ref, o_ref, xn_ref, *, half, scale, n_scaled):
    @pl.when(pl.program_id(1) == 0)
    def _():
        xn_ref[...] = _norm_rows(h_ref[...], g_ref[...]).astype(BF16)

    y = jnp.dot(xn_ref[...], w_ref[...], preferred_element_type=F32)
    y = y * jnp.where(pl.program_id(1) < n_scaled, scale, 1.0)
    c, slo, shi = c_ref[...], slo_ref[...], shi_ref[...]
    n_rb, hb = o_ref.shape[0], o_ref.shape[1]
    for hh in range(hb):
        yh = y[:, hh * LANES:(hh + 1) * LANES]
        if 2 * half == LANES:
            r = yh * c + pltpu.roll(yh, half, axis=1) * (slo + shi)
        else:
            r = yh * c + pltpu.roll(yh, LANES - half, axis=1) * slo + pltpu.roll(yh, half, axis=1) * shi
        r = r.astype(o_ref.dtype)
        for rb in range(n_rb):
            o_ref[rb, hh] = r[rb * LANES:(rb + 1) * LANES]


def rope_proj(h, gain, w, tables, *, rot_dim, scale=1.0, n_scaled=0, tm=1024, hb=4):
    T, D = h.shape
    n_heads = w.shape[1] // LANES
    tm, hb = min(tm, T), min(hb, n_heads)
    tn = hb * LANES
    tbl = pl.BlockSpec((tm, LANES), lambda i, j: (i, 0))
    return pl.pallas_call(
        functools.partial(_rope_proj_kernel, half=rot_dim // 2, scale=scale, n_scaled=n_scaled),
        grid=(T // tm, n_heads // hb),
        in_specs=[
            pl.BlockSpec((tm, D), lambda i, j: (i, 0)),
            pl.BlockSpec((1, D), lambda i, j: (0, 0)),
            pl.BlockSpec((D, tn), lambda i, j: (0, j)),
            tbl, tbl, tbl,
        ],
        out_specs=pl.BlockSpec((tm // LANES, hb, LANES, LANES), lambda i, j: (i, j, 0, 0)),
        out_shape=jax.ShapeDtypeStruct((T // LANES, n_heads, LANES, LANES), BF16),
        scratch_shapes=[pltpu.VMEM((tm, D), BF16)],
        compiler_params=_params("parallel", "arbitrary"),
        name="rope_proj",
    )(h, gain.reshape(1, D), w, *tables)


def _f32_from_ordered_bits(key):
    return pltpu.bitcast(key ^ ((key >> 31) & 0x7FFFFFFF), F32)


def _dsa_kernel(q_ref, iq_ref, iw_ref, k_ref, v_ref, ik_ref, o_ref, score_ref, ot_ref, m_ref, s_ref, *, n_sel):
    i = pl.program_id(0)
    n_chunks = (i * DSA_QB + DSA_QB + DSA_KC - 1) // DSA_KC
    q_pos = i * DSA_QB + lax.broadcasted_iota(I32, (DSA_KC, LANES), 1)
    row_iota = lax.broadcasted_iota(I32, (DSA_KC, LANES), 0)

    iw_t = iw_ref[...].T
    iq2 = iq_ref[0].reshape(IDX_HEADS * DSA_QB, HEAD_DIM)

    def index_chunk(c, carry):
        ikc = ik_ref[pl.ds(c * DSA_KB, DSA_KB), 0].reshape(DSA_KC, HEAD_DIM)
        d = _dot_nt(ikc, iq2)
        acc = jnp.zeros((DSA_KC, LANES), F32)
        for hh in range(IDX_HEADS):
            acc = acc + iw_t[hh:hh + 1, :] * jnp.maximum(d[:, hh * LANES:(hh + 1) * LANES], 0.0)
        causal = c * DSA_KC + row_iota <= q_pos
        score_ref[pl.ds(pl.multiple_of(c * DSA_KC, DSA_KC), DSA_KC), :] = jnp.where(causal, acc, -jnp.inf)
        return carry

    lax.fori_loop(0, n_chunks, index_chunk, 0)

    def count_ge(cand):
        def body(c, acc):
            sc = score_ref[pl.ds(pl.multiple_of(c * DSA_KC, DSA_KC), DSA_KC), :]
            hit = jnp.where(sc >= cand, 1, 0).astype(I32)
            return acc + jnp.sum(hit.reshape(DSA_KC // SUBLANES, SUBLANES, LANES), axis=0)

        acc = lax.fori_loop(0, n_chunks, body, jnp.zeros((SUBLANES, LANES), I32))
        return jnp.sum(acc, axis=0, keepdims=True)

    thr_key = jnp.where(count_ge(jnp.zeros((1, LANES), F32)) >= n_sel, 0, INT32_MIN).astype(I32)

    def bit_step(bi, thr_key):
        cand = thr_key + jnp.left_shift(jnp.int32(1), 30 - bi)
        return jnp.where(count_ge(_f32_from_ordered_bits(cand)) >= n_sel, cand, thr_key)

    thr_key = lax.fori_loop(0, 31, bit_step, thr_key)
    thr = jnp.where(thr_key == INT32_MIN, -jnp.inf, _f32_from_ordered_bits(thr_key))

    m_ref[...] = jnp.full(m_ref.shape, -1e29, F32)
    s_ref[...] = jnp.zeros(s_ref.shape, F32)
    ot_ref[...] = jnp.zeros(ot_ref.shape, F32)

    def attend_chunk(c, carry):
        r0 = pl.multiple_of(c * DSA_KC, DSA_KC)
        sc = score_ref[pl.ds(r0, DSA_KC), :]
        keep = jnp.logical_and(sc >= thr, c * DSA_KC + row_iota <= q_pos)
        bias = jnp.where(keep, 0.0, NEG_BIG)
        bias4 = jnp.concatenate([bias] * ATT_GROUP, axis=1)
        for g in range(ATT_KV_HEADS):
            kc = k_ref[pl.ds(c * DSA_KB, DSA_KB), g].reshape(DSA_KC, HEAD_DIM)
            vc = v_ref[pl.ds(c * DSA_KB, DSA_KB), g].reshape(DSA_KC, HEAD_DIM)
            qg = q_ref[0, g * ATT_GROUP:(g + 1) * ATT_GROUP].reshape(ATT_GROUP * DSA_QB, HEAD_DIM)
            logit = _dot_nt(kc, qg) + bias4
            m_old = m_ref[g]
            m_new = jnp.maximum(m_old, jnp.max(logit, axis=0, keepdims=True))
            alpha = jnp.exp(m_old - m_new)
            p = jnp.exp(logit - m_new)
            s_ref[g] = alpha * s_ref[g] + jnp.sum(p, axis=0, keepdims=True)
            pv = lax.dot_general(vc, p.astype(BF16), (((0,), (0,)), ((), ())), preferred_element_type=F32)
            ot_ref[g] = ot_ref[g] * alpha + pv
            m_ref[g] = m_new
        return carry

    lax.fori_loop(0, n_chunks, attend_chunk, 0)

    for g in range(ATT_KV_HEADS):
        o_t = ot_ref[g] * (1.0 / s_ref[g])
        for hh in range(ATT_GROUP):
            head = g * ATT_GROUP + hh
            o_ref[:, head * HEAD_DIM:(head + 1) * HEAD_DIM] = o_t[:, hh * LANES:(hh + 1) * LANES].T.astype(o_ref.dtype)


def dsa_core(qk, v, iq, ik, iw, n_sel):
    nb = qk.shape[0]
    T = nb * LANES
    resident = lambda heads, first: pl.BlockSpec((nb, heads, LANES, LANES), lambda i: (0, first // heads, 0, 0),
                                                 pipeline_mode=pl.Buffered(1))
    return pl.pallas_call(
        functools.partial(_dsa_kernel, n_sel=n_sel),
        grid=(nb,),
        in_specs=[
            pl.BlockSpec((1, ATT_HEADS, LANES, LANES), lambda i: (i, 0, 0, 0)),
            pl.BlockSpec((1, IDX_HEADS, LANES, LANES), lambda i: (i, 0, 0, 0)),
            pl.BlockSpec((DSA_QB, LANES), lambda i: (i, 0)),
            resident(ATT_KV_HEADS, ATT_HEADS),
            resident(ATT_KV_HEADS, 0),
            resident(1, 0),
        ],
        out_specs=pl.BlockSpec((DSA_QB, ATT_HEADS * HEAD_DIM), lambda i: (i, 0)),
        out_shape=jax.ShapeDtypeStruct((T, ATT_HEADS * HEAD_DIM), BF16),
        scratch_shapes=[
            pltpu.VMEM((T, LANES), F32),
            pltpu.VMEM((ATT_KV_HEADS, HEAD_DIM, ATT_GROUP * DSA_QB), F32),
            pltpu.VMEM((ATT_KV_HEADS, 1, ATT_GROUP * DSA_QB), F32),
            pltpu.VMEM((ATT_KV_HEADS, 1, ATT_GROUP * DSA_QB), F32),
        ],
        compiler_params=_params("parallel"),
        name="dsa_core",
    )(qk, iq, iw, qk, v, ik)


def dsa_layer(h, gain, w_in, w_out):
    T, D = h.shape
    nq, nkv = ATT_HEADS * HEAD_DIM, ATT_KV_HEADS * HEAD_DIM
    ni = IDX_HEADS * HEAD_DIM
    w_qk = w_in[:, :nq + nkv].astype(BF16)
    w_v = w_in[:, nq + nkv:nq + 2 * nkv].astype(BF16)
    o_i = nq + 2 * nkv
    w_iq = w_in[:, o_i:o_i + ni].astype(BF16)
    w_ik = w_in[:, o_i + ni:o_i + ni + HEAD_DIM].astype(BF16)
    w_iw = jnp.zeros((D, LANES), F32).at[:, :IDX_HEADS].set(w_in[:, o_i + ni + HEAD_DIM:]).astype(BF16)
    qk = rope_proj(h, gain, w_qk, _rope_tables(T, HEAD_DIM), rot_dim=HEAD_DIM, scale=HEAD_DIM ** -0.5,
                   n_scaled=ATT_HEADS // 4)
    idx_tables = _rope_tables(T, IDX_ROPE_DIM)
    iq = rope_proj(h, gain, w_iq, idx_tables, rot_dim=IDX_ROPE_DIM)
    ik = rope_proj(h, gain, w_ik, idx_tables, rot_dim=IDX_ROPE_DIM)
    v = norm_proj(h, gain, w_v, out_dtype=BF16).reshape(T // LANES, LANES, ATT_KV_HEADS, HEAD_DIM).transpose(0, 2, 1, 3)
    iw = norm_proj(h, gain, w_iw) * (IDX_HEADS ** -0.5 * HEAD_DIM ** -0.5)
    o = dsa_core(qk, v, iq, ik, iw, min(TOPK_MAX, T // 4))
    return matmul_residual(o, w_out, h)


MOE_TILE = 256
MOE_EXPERT_VMEM_LIMIT_BYTES = 62 * 1024 * 1024


SLAB_LINES = D_MODEL // LANES
SLAB_PITCH = 20


def _store_slabs(ref, x):
    n = x.shape[0]
    for c in range(SLAB_LINES):
        ref[pl.ds(c, n, stride=SLAB_PITCH), :] = x[:, c * LANES:(c + 1) * LANES]
    for c in range(SLAB_LINES, SLAB_PITCH):
        ref[pl.ds(c, n, stride=SLAB_PITCH), :] = jnp.zeros((n, LANES), x.dtype)


def _load_slabs(ref, first_slab, n):
    return jnp.concatenate(
        [ref[pl.ds(first_slab * SLAB_PITCH + c, n, stride=SLAB_PITCH), :] for c in range(SLAB_LINES)], axis=1)


def _router_kernel(h_ref, g_ref, rw_ref, rb_ref, xn_ref, idx_ref, gate_ref):
    xn = _norm_rows(h_ref[...], g_ref[...])
    _store_slabs(xn_ref, xn)
    logits = jnp.dot(xn, rw_ref[...], preferred_element_type=F32, precision=lax.Precision.HIGHEST) + rb_ref[...]
    lane = lax.broadcasted_iota(I32, logits.shape, 1)
    vals, idxs = [], []
    for _ in range(MOE_TOP_K):
        m = jnp.max(logits, axis=-1, keepdims=True)
        sel = jnp.min(jnp.where(logits == m, lane, LANES), axis=-1, keepdims=True)
        vals.append(m)
        idxs.append(sel)
        logits = jnp.where(lane == sel, -jnp.inf, logits)
    ex = [jnp.exp(v - vals[0]) for v in vals]
    den = ex[0] + ex[1] + ex[2] + ex[3]
    idx_out = jnp.zeros(lane.shape, I32)
    gate_out = jnp.zeros(lane.shape, F32)
    for k in range(MOE_TOP_K):
        idx_out = jnp.where(lane == k, idxs[k], idx_out)
        gate_out = jnp.where(lane == k, ex[k] / den, gate_out)
    idx_ref[...] = idx_out
    gate_ref[...] = gate_out


def moe_router(h, gain, router_w, router_b, *, tm=256):
    T, D = h.shape
    rw = jnp.zeros((D, LANES), F32).at[:, :N_EXPERTS].set(router_w)
    rb = jnp.full((1, LANES), -jnp.inf, F32).at[0, :N_EXPERTS].set(router_b)
    return pl.pallas_call(
        _router_kernel,
        grid=(T // tm,),
        in_specs=[
            pl.BlockSpec((tm, D), lambda i: (i, 0)),
            pl.BlockSpec((1, D), lambda i: (0, 0)),
            pl.BlockSpec((D, LANES), lambda i: (0, 0)),
            pl.BlockSpec((1, LANES), lambda i: (0, 0)),
        ],
        out_specs=[
            pl.BlockSpec((tm * SLAB_PITCH, LANES), lambda i: (i, 0)),
            pl.BlockSpec((tm, LANES), lambda i: (i, 0)),
            pl.BlockSpec((tm, LANES), lambda i: (i, 0)),
        ],
        out_shape=[
            jax.ShapeDtypeStruct((T * SLAB_PITCH, LANES), F32),
            jax.ShapeDtypeStruct((T, LANES), I32),
            jax.ShapeDtypeStruct((T, LANES), F32),
        ],
        compiler_params=_params("parallel"),
        name="moe_router",
    )(h, gain.reshape(1, D), rw, rb)


def _route(idx, n_slots, tile):
    T = idx.shape[0]
    MOE_TILE = tile
    n_tiles = n_slots // MOE_TILE
    flat_e = idx.reshape(-1)
    onehot = (flat_e[:, None] == jnp.arange(N_EXPERTS, dtype=I32)[None, :]).astype(I32)
    csum = jnp.cumsum(onehot, axis=0)
    rank = jnp.sum(csum * onehot, axis=1) - 1
    counts = csum[-1]
    padded = ((counts + MOE_TILE - 1) // MOE_TILE) * MOE_TILE
    ends = jnp.cumsum(padded)
    starts = ends - padded
    slot = jnp.sum(starts[None, :] * onehot, axis=1) + rank
    token = jnp.arange(T * MOE_TOP_K, dtype=I32) // MOE_TOP_K
    tok_of_slot = jnp.zeros((n_slots,), I32).at[slot].set(token, unique_indices=True)
    tile_start = jnp.arange(n_tiles, dtype=I32) * MOE_TILE
    tile_expert = jnp.sum((tile_start[:, None] >= ends[None, :]).astype(I32), axis=1)
    tile_expert = jnp.minimum(tile_expert, N_EXPERTS - 1)
    n_used = ends[-1] // MOE_TILE
    e_ids = jnp.arange(N_EXPERTS, dtype=I32)
    used = counts > 0
    later_used = jnp.logical_and(e_ids[None, :] > e_ids[:, None], used[None, :])
    next_used = jnp.min(jnp.where(later_used, e_ids[None, :], N_EXPERTS), axis=1)
    next_used = jnp.where(next_used == N_EXPERTS, -1, next_used)
    buffer_of = (jnp.cumsum(used.astype(I32)) - 1) % 2
    is_first = jnp.concatenate([jnp.ones((1,), I32), (tile_expert[1:] != tile_expert[:-1]).astype(I32)])
    plan = jnp.stack([tile_expert, is_first, next_used[tile_expert], buffer_of[tile_expert]]).astype(I32)
    return tok_of_slot, slot.reshape(T, MOE_TOP_K), plan, n_used.reshape(1)


def _start_slab_gather(ids_ref, n_rows, src_hbm, dst_ref, sem, *, both_queues=False):
    for r in range(n_rows):
        line = ids_ref[0, 0, r]
        copy = pltpu.make_async_copy(src_hbm.at[pl.ds(line, SLAB_LINES), :],
                                     dst_ref.at[pl.ds(r * SLAB_PITCH, SLAB_LINES), :], sem)
        copy.start(priority=r % 2 if both_queues else 0)


def _wait_slab_gather(n_rows, src_hbm, dst_ref, sem):
    n = n_rows * SLAB_LINES
    pltpu.make_async_copy(src_hbm.at[pl.ds(0, n), :], dst_ref.at[pl.ds(0, n), :], sem).wait()


def _weight_copies(expert, buf, w1_hbm, w2_hbm, w1buf, w2buf, wsem):
    return (pltpu.make_async_copy(w1_hbm.at[expert], w1buf.at[buf], wsem.at[0, buf]),
            pltpu.make_async_copy(w2_hbm.at[expert], w2buf.at[buf], wsem.at[1, buf]))


def _expert_kernel(plan_ref, nused_ref, ids_cur_ref, ids_nxt_ref, x_hbm, w1_hbm, b1_ref, w2_hbm, b2_ref,
                   y_ref, xbuf, sem, w1buf, w2buf, wsem, *, tile):
    t = pl.program_id(0)
    n_used = nused_ref[0]
    slot = t % 2
    expert, is_first, next_expert, buf = plan_ref[0, t], plan_ref[1, t], plan_ref[2, t], plan_ref[3, t]
    active = t < n_used
    weights = functools.partial(_weight_copies, w1_hbm=w1_hbm, w2_hbm=w2_hbm, w1buf=w1buf, w2buf=w2buf, wsem=wsem)

    @pl.when(jnp.logical_and(t == 0, active))
    def _():
        for copy in weights(expert, buf):
            copy.start()
        _start_slab_gather(ids_cur_ref, tile, x_hbm, xbuf.at[0], sem.at[0])

    @pl.when(jnp.logical_and(active, jnp.logical_and(is_first == 1, next_expert >= 0)))
    def _():
        for copy in weights(next_expert, 1 - buf):
            copy.start()

    @pl.when(t + 1 < n_used)
    def _():
        _start_slab_gather(ids_nxt_ref, tile, x_hbm, xbuf.at[1 - slot], sem.at[1 - slot])

    @pl.when(active)
    def _():
        @pl.when(is_first == 1)
        def _():
            for copy in weights(expert, buf):
                copy.wait()

        _wait_slab_gather(tile, x_hbm, xbuf.at[slot], sem.at[slot])
        x = _load_slabs(xbuf.at[slot], 0, tile).astype(BF16)
        hcat = jnp.dot(x, w1buf[buf].astype(BF16), preferred_element_type=F32) + b1_ref[0]
        gate = jnp.minimum(hcat[:, :D_EXPERT], SWIGLU_LIMIT)
        up = jnp.clip(hcat[:, D_EXPERT:], -SWIGLU_LIMIT, SWIGLU_LIMIT)
        glu = gate * _sigmoid(SWIGLU_ALPHA * gate)
        act = ((up + 1.0) * glu).astype(BF16)
        _store_slabs(y_ref, jnp.dot(act, w2buf[buf].astype(BF16), preferred_element_type=F32) + b2_ref[0])

    @pl.when(jnp.logical_not(active))
    def _():
        y_ref[...] = jnp.zeros(y_ref.shape, F32)


def moe_experts(xn_slabs, tok_of_slot, plan, n_used, w1, b1, w2, b2, tile):
    D = D_MODEL
    n_slots = tok_of_slot.shape[0]
    n_tiles = n_slots // tile
    ids = (tok_of_slot * SLAB_PITCH).reshape(n_tiles, 1, tile)
    grid_spec = pltpu.PrefetchScalarGridSpec(
        num_scalar_prefetch=2,
        grid=(n_tiles,),
        in_specs=[
            pl.BlockSpec((1, 1, tile), lambda t, pn, nu: (t, 0, 0), memory_space=pltpu.SMEM),
            pl.BlockSpec((1, 1, tile), lambda t, pn, nu: (jnp.minimum(t + 1, n_tiles - 1), 0, 0),
                         memory_space=pltpu.SMEM),
            pl.BlockSpec(memory_space=pl.ANY),
            pl.BlockSpec(memory_space=pl.ANY),
            pl.BlockSpec((1, 1, 2 * D_EXPERT), lambda t, pn, nu: (pn[0, t], 0, 0)),
            pl.BlockSpec(memory_space=pl.ANY),
            pl.BlockSpec((1, 1, D), lambda t, pn, nu: (pn[0, t], 0, 0)),
        ],
        out_specs=pl.BlockSpec((tile * SLAB_PITCH, LANES), lambda t, pn, nu: (t, 0)),
        scratch_shapes=[
            pltpu.VMEM((2, tile * SLAB_PITCH, LANES), F32),
            pltpu.SemaphoreType.DMA((2,)),
            pltpu.VMEM((2, D, 2 * D_EXPERT), F32),
            pltpu.VMEM((2, D_EXPERT, D), F32),
            pltpu.SemaphoreType.DMA((2, 2)),
        ],
    )
    return pl.pallas_call(
        functools.partial(_expert_kernel, tile=tile),
        grid_spec=grid_spec,
        out_shape=jax.ShapeDtypeStruct((n_slots * SLAB_PITCH, LANES), F32),
        compiler_params=pltpu.CompilerParams(dimension_semantics=("arbitrary",),
                                             vmem_limit_bytes=MOE_EXPERT_VMEM_LIMIT_BYTES),
        name="moe_experts",
    )(plan, n_used, ids, ids, xn_slabs, w1, b1.reshape(N_EXPERTS, 1, -1), w2, b2.reshape(N_EXPERTS, 1, -1))


COMBINE_TOKENS = 128


def _combine_kernel(ids_cur_ref, ids_nxt_ref, y_hbm, gate_ref, h_ref, *rest):
    out_gain_ref = rest[0] if len(rest) == 4 else None
    o_ref, ybuf, sem = rest[-3:]
    t = pl.program_id(0)
    n = pl.num_programs(0)
    slot = t % 2
    rows = COMBINE_TOKENS * MOE_TOP_K

    @pl.when(t == 0)
    def _():
        _start_slab_gather(ids_cur_ref, rows, y_hbm, ybuf.at[0], sem.at[0], both_queues=True)

    @pl.when(t + 1 < n)
    def _():
        _start_slab_gather(ids_nxt_ref, rows, y_hbm, ybuf.at[1 - slot], sem.at[1 - slot], both_queues=True)

    _wait_slab_gather(rows, y_hbm, ybuf.at[slot], sem.at[slot])
    gates = gate_ref[...]
    acc = h_ref[...]
    for k in range(MOE_TOP_K):
        acc = acc + gates[:, k:k + 1] * _load_slabs(ybuf.at[slot], k * COMBINE_TOKENS, COMBINE_TOKENS)
    o_ref[...] = acc if out_gain_ref is None else _norm_rows(acc, out_gain_ref[...])


def moe_combine(y_slabs, slot_of, gate_pad, h, out_gain=None):
    T, D = h.shape
    gain_args = [] if out_gain is None else [out_gain.reshape(1, D)]
    gain_specs = [] if out_gain is None else [pl.BlockSpec((1, D), lambda t: (0, 0))]
    n_steps = T // COMBINE_TOKENS
    rows = COMBINE_TOKENS * MOE_TOP_K
    ids = (slot_of * SLAB_PITCH).reshape(n_steps, COMBINE_TOKENS, MOE_TOP_K).transpose(0, 2, 1)
    ids = ids.reshape(n_steps, 1, rows)
    return pl.pallas_call(
        _combine_kernel,
        grid=(n_steps,),
        in_specs=[
            pl.BlockSpec((1, 1, rows), lambda t: (t, 0, 0), memory_space=pltpu.SMEM),
            pl.BlockSpec((1, 1, rows), lambda t: (jnp.minimum(t + 1, n_steps - 1), 0, 0), memory_space=pltpu.SMEM),
            pl.BlockSpec(memory_space=pl.ANY),
            pl.BlockSpec((COMBINE_TOKENS, LANES), lambda t: (t, 0)),
            pl.BlockSpec((COMBINE_TOKENS, D), lambda t: (t, 0)),
        ] + gain_specs,
        out_specs=pl.BlockSpec((COMBINE_TOKENS, D), lambda t: (t, 0)),
        out_shape=jax.ShapeDtypeStruct((T, D), F32),
        scratch_shapes=[pltpu.VMEM((2, rows * SLAB_PITCH, LANES), F32), pltpu.SemaphoreType.DMA((2,))],
        compiler_params=_params("arbitrary"),
        name="moe_combine",
    )(ids, ids, y_slabs, gate_pad, h, *gain_args)


def moe_layer(h, gain, router_w, router_b, w1, b1, w2, b2, out_gain=None):
    T = h.shape[0]
    n_slots = T * MOE_TOP_K + N_EXPERTS * MOE_TILE
    xn_slabs, idx_pad, gate_pad = moe_router(h, gain, router_w, router_b)
    tok_of_slot, slot_of, plan, n_used = _route(idx_pad[:, :MOE_TOP_K], n_slots, MOE_TILE)
    y_slabs = moe_experts(xn_slabs, tok_of_slot, plan, n_used, w1, b1, w2, b2, MOE_TILE)
    return moe_combine(y_slabs, slot_of, gate_pad, h, out_gain)


def kernel(x, hgrn_lb_logits,
           l0_norm_mix, l0_conv_in, l0_conv_w, l0_conv_out, l0_norm_ffn,
           l0_router_w, l0_router_b, l0_exp_w1, l0_exp_b1, l0_exp_w2, l0_exp_b2,
           l1_norm_mix, l1_hgrn_in, l1_hgrn_gnorm, l1_hgrn_out, l1_norm_ffn,
           l1_router_w, l1_router_b, l1_exp_w1, l1_exp_b1, l1_exp_w2, l1_exp_b2,
           l2_norm_mix, l2_dsa_in, l2_dsa_out, l2_norm_ffn,
           l2_router_w, l2_router_b, l2_exp_w1, l2_exp_b1, l2_exp_w2, l2_exp_b2,
           l3_norm_mix, l3_conv_in, l3_conv_w, l3_conv_out, l3_norm_ffn,
           l3_router_w, l3_router_b, l3_exp_w1, l3_exp_b1, l3_exp_w2, l3_exp_b2,
           final_norm_gain):
    B, T, D = x.shape
    bf = lambda w: w.astype(BF16)
    h = x.reshape(B * T, D)

    h = short_conv_layer(h, l0_norm_mix, bf(l0_conv_in), l0_conv_w, bf(l0_conv_out))
    h = moe_layer(h, l0_norm_ffn, l0_router_w, l0_router_b, l0_exp_w1, l0_exp_b1, l0_exp_w2, l0_exp_b2)

    lb_soft = jax.nn.softmax(hgrn_lb_logits.astype(F32), axis=0)
    lb_all = jnp.cumsum(lb_soft, axis=0) - lb_soft[0:1]
    h = hgrn2_layer(h, l1_norm_mix, bf(l1_hgrn_in), l1_hgrn_gnorm, bf(l1_hgrn_out), lb_all[1])
    h = moe_layer(h, l1_norm_ffn, l1_router_w, l1_router_b, l1_exp_w1, l1_exp_b1, l1_exp_w2, l1_exp_b2)

    h = dsa_layer(h, l2_norm_mix, l2_dsa_in, bf(l2_dsa_out))
    h = moe_layer(h, l2_norm_ffn, l2_router_w, l2_router_b, l2_exp_w1, l2_exp_b1, l2_exp_w2, l2_exp_b2)

    h = short_conv_layer(h, l3_norm_mix, bf(l3_conv_in), l3_conv_w, bf(l3_conv_out))
    out = moe_layer(h, l3_norm_ffn, l3_router_w, l3_router_b, l3_exp_w1, l3_exp_b1, l3_exp_w2, l3_exp_b2,
                    out_gain=final_norm_gain)
    return out.reshape(B, T, D)
```

```python
import functools

import numpy as np
import jax
import jax.numpy as jnp
from jax import lax
from jax.experimental import pallas as pl
from jax.experimental.pallas import tpu as pltpu

F32 = jnp.float32
BF16 = jnp.bfloat16
I32 = jnp.int32

D_MODEL = 2048
RMS_EPS = 1e-6
ROPE_THETA = 10000.0

HG_HEADS = 16
HG_DK = 128

ATT_HEADS = 16
ATT_KV_HEADS = 4
ATT_GROUP = ATT_HEADS // ATT_KV_HEADS
HEAD_DIM = 128
IDX_HEADS = 16
IDX_ROPE_DIM = 64
TOPK_MAX = 256

N_EXPERTS = 32
MOE_TOP_K = 4
D_EXPERT = D_MODEL // 2
SWIGLU_LIMIT = 7.0
SWIGLU_ALPHA = 1.702

LANES = 128
SUBLANES = 8
VMEM_LIMIT_BYTES = 56 * 1024 * 1024
NEG_BIG = -1e30


def _params(*sem):
    return pltpu.CompilerParams(dimension_semantics=sem, vmem_limit_bytes=VMEM_LIMIT_BYTES)


def _norm_rows(x, g):
    ms = jnp.mean(x * x, axis=-1, keepdims=True)
    return x * lax.rsqrt(ms + RMS_EPS) * g


def _sigmoid(x):
    return 1.0 / (1.0 + jnp.exp(-x))


def _proj_kernel(h_ref, g_ref, w_ref, o_ref, xn_ref):
    @pl.when(pl.program_id(1) == 0)
    def _():
        xn_ref[...] = _norm_rows(h_ref[...], g_ref[...]).astype(BF16)

    o_ref[...] = jnp.dot(xn_ref[...], w_ref[...], preferred_element_type=F32).astype(o_ref.dtype)


def norm_proj(h, gain, w, *, tm=1024, tn=1024, out_dtype=F32):
    T, D = h.shape
    N = w.shape[1]
    tm, tn = min(tm, T), min(tn, N)
    return pl.pallas_call(
        _proj_kernel,
        grid=(T // tm, N // tn),
        in_specs=[
            pl.BlockSpec((tm, D), lambda i, j: (i, 0)),
            pl.BlockSpec((1, D), lambda i, j: (0, 0)),
            pl.BlockSpec((D, tn), lambda i, j: (0, j)),
        ],
        out_specs=pl.BlockSpec((tm, tn), lambda i, j: (i, j)),
        out_shape=jax.ShapeDtypeStruct((T, N), out_dtype),
        scratch_shapes=[pltpu.VMEM((tm, D), BF16)],
        compiler_params=_params("parallel", "arbitrary"),
        name="norm_proj",
    )(h, gain.reshape(1, D), w)


def _matmul_res_kernel(a_ref, w_ref, h_ref, o_ref):
    o_ref[...] = h_ref[...] + jnp.dot(a_ref[...], w_ref[...], preferred_element_type=F32)


def matmul_residual(a, w, h, *, tm=1024, tn=1024):
    T, K = a.shape
    N = w.shape[1]
    tm, tn = min(tm, T), min(tn, N)
    return pl.pallas_call(
        _matmul_res_kernel,
        grid=(T // tm, N // tn),
        in_specs=[
            pl.BlockSpec((tm, K), lambda i, j: (i, 0)),
            pl.BlockSpec((K, tn), lambda i, j: (0, j)),
            pl.BlockSpec((tm, tn), lambda i, j: (i, j)),
        ],
        out_specs=pl.BlockSpec((tm, tn), lambda i, j: (i, j)),
        out_shape=jax.ShapeDtypeStruct((T, N), F32),
        compiler_params=_params("parallel", "arbitrary"),
        name="matmul_residual",
    )(a, w, h)


def _conv_in_kernel(h_ref, g_ref, wb_ref, wc_ref, wh_ref, b_ref, u_ref, xn_ref):
    @pl.when(pl.program_id(1) == 0)
    def _():
        xn_ref[...] = _norm_rows(h_ref[...], g_ref[...]).astype(BF16)

    xn = xn_ref[...]
    b_ref[...] = jnp.dot(xn, wb_ref[...], preferred_element_type=F32)
    c = jnp.dot(xn, wc_ref[...], preferred_element_type=F32)
    hh = jnp.dot(xn, wh_ref[...], preferred_element_type=F32)
    u_ref[...] = c * hh


def conv_in_proj(h, gain, w_in, *, tm=1024, tn=512):
    T, D = h.shape
    tm = min(tm, T)
    nb = D // tn
    out = jax.ShapeDtypeStruct((T, D), F32)
    return pl.pallas_call(
        _conv_in_kernel,
        grid=(T // tm, nb),
        in_specs=[
            pl.BlockSpec((tm, D), lambda i, j: (i, 0)),
            pl.BlockSpec((1, D), lambda i, j: (0, 0)),
            pl.BlockSpec((D, tn), lambda i, j: (0, j)),
            pl.BlockSpec((D, tn), lambda i, j: (0, j + nb)),
            pl.BlockSpec((D, tn), lambda i, j: (0, j + 2 * nb)),
        ],
        out_specs=[pl.BlockSpec((tm, tn), lambda i, j: (i, j))] * 2,
        out_shape=[out, out],
        scratch_shapes=[pltpu.VMEM((tm, D), BF16)],
        compiler_params=_params("parallel", "arbitrary"),
        name="conv_in_proj",
    )(h, gain.reshape(1, D), w_in, w_in, w_in)


def _shift_rows(u, prev, s):
    r = pltpu.roll(u, s, axis=0)
    p = pltpu.roll(prev, s, axis=0)
    row = lax.broadcasted_iota(I32, p.shape, 0)
    head = jnp.where(row < s, p, r[:SUBLANES])
    return jnp.concatenate([head, r[SUBLANES:]], axis=0)


def _conv_gate_kernel(u_ref, up_ref, b_ref, cw_ref, y_ref):
    u = u_ref[...]
    prev = jnp.where(pl.program_id(0) > 0, up_ref[...], 0.0)
    cw = cw_ref[...]
    conv = cw[2:3] * u + cw[1:2] * _shift_rows(u, prev, 1) + cw[0:1] * _shift_rows(u, prev, 2)
    y_ref[...] = (b_ref[...] * conv).astype(y_ref.dtype)


def conv_gate(u, b, conv_w, *, tm=512, tn=1024):
    T, D = u.shape
    rb = tm // SUBLANES
    return pl.pallas_call(
        _conv_gate_kernel,
        grid=(T // tm, D // tn),
        in_specs=[
            pl.BlockSpec((tm, tn), lambda i, j: (i, j)),
            pl.BlockSpec((SUBLANES, tn), lambda i, j: (jnp.maximum(i * rb - 1, 0), j)),
            pl.BlockSpec((tm, tn), lambda i, j: (i, j)),
            pl.BlockSpec((3, tn), lambda i, j: (0, j)),
        ],
        out_specs=pl.BlockSpec((tm, tn), lambda i, j: (i, j)),
        out_shape=jax.ShapeDtypeStruct((T, D), BF16),
        compiler_params=_params("parallel", "parallel"),
        name="conv_gate",
    )(u, u, b, conv_w)


def short_conv_layer(h, gain, w_in, conv_w, w_out):
    b, u = conv_in_proj(h, gain, w_in)
    y = conv_gate(u, b, conv_w)
    return matmul_residual(y, w_out, h)


HG_ROWS = 512
HG_SUB = 8
HG_LEVELS = 6
HG_DIAG = 128


def _hgrn_level_table(n):
    tb = np.arange(n)[:, None] // HG_SUB
    sb = np.arange(n)[None, :] // HG_SUB
    x = np.maximum(tb ^ sb, 1)
    lvl = np.floor(np.log2(x)).astype(np.int32)
    return np.where(tb > sb, lvl, -1).astype(np.int32)


def _dot_nt(a, b):
    return lax.dot_general(a, b, (((1,), (1,)), ((), ())), preferred_element_type=F32)


def _hgrn_kernel(q_ref, f_ref, i_ref, g_ref, lb_ref, gn_ref, lvl_ref, o_ref, st_ref, qs, ks, bs, vs, os_):
    R = HG_ROWS

    @pl.when(pl.program_id(1) == 0)
    def _():
        st_ref[...] = jnp.zeros(st_ref.shape, F32)

    lb = lb_ref[...]
    qr = q_ref[...]
    q = qr * _sigmoid(qr)
    forget = lb + (1.0 - lb) * _sigmoid(f_ref[...])
    k = 1.0 - forget
    v = i_ref[...]
    row = lax.broadcasted_iota(I32, (R, LANES), 0)
    b = jnp.log(forget)
    step = 1
    while step < R:
        b = b + jnp.where(row >= step, pltpu.roll(b, step, axis=0), 0.0)
        step *= 2

    st = st_ref[...]
    o = _dot_nt((q * jnp.exp(b)).astype(BF16), st.astype(BF16))

    lvl = lvl_ref[...]
    n_diag = R // HG_DIAG
    diag = [jnp.zeros((HG_DIAG, HG_DIAG), F32)] * n_diag
    below = {}
    for p in range(HG_LEVELS):
        half = HG_SUB << p
        blk = 2 * half
        bref = jnp.concatenate(
            [jnp.broadcast_to(b[j * blk + half - 1:j * blk + half], (blk, LANES)) for j in range(R // blk)], axis=0)
        qt = (q * jnp.exp(jnp.minimum(b - bref, 0.0))).astype(BF16)
        kt = (k * jnp.exp(jnp.minimum(bref - b, 0.0))).astype(BF16)
        if blk <= HG_DIAG:
            for d in range(n_diag):
                rows = slice(d * HG_DIAG, (d + 1) * HG_DIAG)
                diag[d] = diag[d] + jnp.where(lvl == p, _dot_nt(qt[rows], kt[rows]), 0.0)
        else:
            for j in range(R // blk):
                below[(blk, j)] = _dot_nt(qt[j * blk + half:(j + 1) * blk], kt[j * blk:j * blk + half])

    def score_block(first, size):
        if size == HG_DIAG:
            return diag[first // HG_DIAG]
        half = size // 2
        top = jnp.concatenate([score_block(first, half), jnp.zeros((half, half), F32)], axis=1)
        bottom = jnp.concatenate([below[(size, first // size)], score_block(first + half, half)], axis=1)
        return jnp.concatenate([top, bottom], axis=0)

    o = o + jnp.dot(score_block(0, R).astype(BF16), v.astype(BF16), preferred_element_type=F32)

    qs[...] = q
    ks[...] = k
    bs[...] = b
    vs[...] = v
    os_[...] = o
    row8 = lax.broadcasted_iota(I32, (HG_SUB, LANES), 0)

    def group(gi, carry):
        r0 = pl.multiple_of(gi * HG_SUB, HG_SUB)
        qb = qs[pl.ds(r0, HG_SUB), :]
        kb = ks[pl.ds(r0, HG_SUB), :]
        bb = bs[pl.ds(r0, HG_SUB), :]
        vb = vs[pl.ds(r0, HG_SUB), :]
        acc = jnp.zeros((HG_SUB, LANES), F32)
        for s in range(HG_SUB):
            decay = jnp.exp(jnp.where(row8 >= s, bb - bb[s:s + 1], -jnp.inf))
            a = jnp.sum(qb * kb[s:s + 1] * decay, axis=-1, keepdims=True)
            acc = acc + a * vb[s:s + 1]
        os_[pl.ds(r0, HG_SUB), :] += acc
        return carry

    lax.fori_loop(0, R // HG_SUB, group, 0, unroll=32)

    b_last = b[R - 1:R]
    kd = (k * jnp.exp(b_last - b)).astype(BF16)
    st_ref[...] = st * jnp.exp(b_last) + jnp.dot(v.T.astype(BF16), kd, preferred_element_type=F32)

    o = os_[...]
    o = o * lax.rsqrt(jnp.mean(o * o, axis=-1, keepdims=True) + RMS_EPS) * gn_ref[...]
    o_ref[...] = (o * _sigmoid(g_ref[...])).astype(o_ref.dtype)


def hgrn2_core(z, lb, gnorm):
    T = z.shape[0]
    D = z.shape[1] // 4
    R = HG_ROWS
    nh = D // LANES
    lvl = jnp.asarray(_hgrn_level_table(HG_DIAG))
    blk = lambda off: pl.BlockSpec((R, LANES), lambda hd, r: (r, off * nh + hd))
    return pl.pallas_call(
        _hgrn_kernel,
        grid=(nh, T // R),
        in_specs=[
            blk(0), blk(1), blk(2), blk(3),
            pl.BlockSpec((1, LANES), lambda hd, r: (0, hd)),
            pl.BlockSpec((1, LANES), lambda hd, r: (0, 0)),
            pl.BlockSpec((HG_DIAG, HG_DIAG), lambda hd, r: (0, 0)),
        ],
        out_specs=pl.BlockSpec((R, LANES), lambda hd, r: (r, hd)),
        out_shape=jax.ShapeDtypeStruct((T, D), BF16),
        scratch_shapes=[pltpu.VMEM((LANES, LANES), F32)] + [pltpu.VMEM((R, LANES), F32)] * 5,
        compiler_params=_params("parallel", "arbitrary"),
        name="hgrn2_core",
    )(z, z, z, z, lb.reshape(1, D), gnorm.reshape(1, LANES), lvl)


def hgrn2_layer(h, gain, w_in, gnorm, w_out, lb):
    z = norm_proj(h, gain, w_in)
    og = hgrn2_core(z, lb, gnorm)
    return matmul_residual(og, w_out, h)


DSA_QB = 128
DSA_KC = 512
DSA_KB = DSA_KC // LANES
INT32_MIN = -(2 ** 31)


def _rope_tables(T, rot_dim):
    half = rot_dim // 2
    inv_freq = 1.0 / (ROPE_THETA ** (jnp.arange(0, rot_dim, 2, dtype=F32) / rot_dim))
    ang = jnp.arange(T, dtype=F32)[:, None] * inv_freq[None, :]
    cos, sin = jnp.cos(ang), jnp.sin(ang)
    ones = jnp.ones((T, HEAD_DIM - rot_dim), F32)
    zeros_h = jnp.zeros((T, half), F32)
    zeros_r = jnp.zeros((T, HEAD_DIM - rot_dim), F32)
    c = jnp.concatenate([cos, cos, ones], axis=1)
    s_lo = jnp.concatenate([-sin, zeros_h, zeros_r], axis=1)
    s_hi = jnp.concatenate([zeros_h, sin, zeros_r], axis=1)
    return c, s_lo, s_hi


def _rope_proj_kernel(h_ref, g_ref, w_ref, c_ref, slo_ref, shi_ref, o_ref, xn_ref, *, half, scale, n_scaled):
    @pl.when(pl.program_id(1) == 0)
    def _():
        xn_ref[...] = _norm_rows(h_ref[...], g_ref[...]).astype(BF16)

    y = jnp.dot(xn_ref[...], w_ref[...], preferred_element_type=F32)
    y = y * jnp.where(pl.program_id(1) < n_scaled, scale, 1.0)
    c, slo, shi = c_ref[...], slo_ref[...], shi_ref[...]
    n_rb, hb = o_ref.shape[0], o_ref.shape[1]
    for hh in range(hb):
        yh = y[:, hh * LANES:(hh + 1) * LANES]
        if 2 * half == LANES:
            r = yh * c + pltpu.roll(yh, half, axis=1) * (slo + shi)
        else:
            r = yh * c + pltpu.roll(yh, LANES - half, axis=1) * slo + pltpu.roll(yh, half, axis=1) * shi
        r = r.astype(o_ref.dtype)
        for rb in range(n_rb):
            o_ref[rb, hh] = r[rb * LANES:(rb + 1) * LANES]


def rope_proj(h, gain, w, tables, *, rot_dim, scale=1.0, n_scaled=0, tm=1024, hb=4):
    T, D = h.shape
    n_heads = w.shape[1] // LANES
    tm, hb = min(tm, T), min(hb, n_heads)
    tn = hb * LANES
    tbl = pl.BlockSpec((tm, LANES), lambda i, j: (i, 0))
    return pl.pallas_call(
        functools.partial(_rope_proj_kernel, half=rot_dim // 2, scale=scale, n_scaled=n_scaled),
        grid=(T // tm, n_heads // hb),
        in_specs=[
            pl.BlockSpec((tm, D), lambda i, j: (i, 0)),
            pl.BlockSpec((1, D), lambda i, j: (0, 0)),
            pl.BlockSpec((D, tn), lambda i, j: (0, j)),
            tbl, tbl, tbl,
        ],
        out_specs=pl.BlockSpec((tm // LANES, hb, LANES, LANES), lambda i, j: (i, j, 0, 0)),
        out_shape=jax.ShapeDtypeStruct((T // LANES, n_heads, LANES, LANES), BF16),
        scratch_shapes=[pltpu.VMEM((tm, D), BF16)],
        compiler_params=_params("parallel", "arbitrary"),
        name="rope_proj",
    )(h, gain.reshape(1, D), w, *tables)


def _f32_from_ordered_bits(key):
    return pltpu.bitcast(key ^ ((key >> 31) & 0x7FFFFFFF), F32)


def _dsa_kernel(q_ref, iq_ref, iw_ref, k_ref, v_ref, ik_ref, o_ref, score_ref, ot_ref, m_ref, s_ref, *, n_sel):
    i = pl.program_id(0)
    n_chunks = (i * DSA_QB + DSA_QB + DSA_KC - 1) // DSA_KC
    q_pos = i * DSA_QB + lax.broadcasted_iota(I32, (DSA_KC, LANES), 1)
    row_iota = lax.broadcasted_iota(I32, (DSA_KC, LANES), 0)

    iw_t = iw_ref[...].T
    iq2 = iq_ref[0].reshape(IDX_HEADS * DSA_QB, HEAD_DIM)

    def index_chunk(c, carry):
        ikc = ik_ref[pl.ds(c * DSA_KB, DSA_KB), 0].reshape(DSA_KC, HEAD_DIM)
        d = _dot_nt(ikc, iq2)
        acc = jnp.zeros((DSA_KC, LANES), F32)
        for hh in range(IDX_HEADS):
            acc = acc + iw_t[hh:hh + 1, :] * jnp.maximum(d[:, hh * LANES:(hh + 1) * LANES], 0.0)
        causal = c * DSA_KC + row_iota <= q_pos
        score_ref[pl.ds(pl.multiple_of(c * DSA_KC, DSA_KC), DSA_KC), :] = jnp.where(causal, acc, -jnp.inf)
        return carry

    lax.fori_loop(0, n_chunks, index_chunk, 0)

    def count_ge(cand):
        def body(c, acc):
            sc = score_ref[pl.ds(pl.multiple_of(c * DSA_KC, DSA_KC), DSA_KC), :]
            hit = jnp.where(sc >= cand, 1, 0).astype(I32)
            return acc + jnp.sum(hit.reshape(DSA_KC // SUBLANES, SUBLANES, LANES), axis=0)

        acc = lax.fori_loop(0, n_chunks, body, jnp.zeros((SUBLANES, LANES), I32))
        return jnp.sum(acc, axis=0, keepdims=True)

    thr_key = jnp.where(count_ge(jnp.zeros((1, LANES), F32)) >= n_sel, 0, INT32_MIN).astype(I32)

    def bit_step(bi, thr_key):
        cand = thr_key + jnp.left_shift(jnp.int32(1), 30 - bi)
        return jnp.where(count_ge(_f32_from_ordered_bits(cand)) >= n_sel, cand, thr_key)

    thr_key = lax.fori_loop(0, 31, bit_step, thr_key)
    thr = jnp.where(thr_key == INT32_MIN, -jnp.inf, _f32_from_ordered_bits(thr_key))

    m_ref[...] = jnp.full(m_ref.shape, -1e29, F32)
    s_ref[...] = jnp.zeros(s_ref.shape, F32)
    ot_ref[...] = jnp.zeros(ot_ref.shape, F32)

    def attend_chunk(c, carry):
        r0 = pl.multiple_of(c * DSA_KC, DSA_KC)
        sc = score_ref[pl.ds(r0, DSA_KC), :]
        keep = jnp.logical_and(sc >= thr, c * DSA_KC + row_iota <= q_pos)
        bias = jnp.where(keep, 0.0, NEG_BIG)
        bias4 = jnp.concatenate([bias] * ATT_GROUP, axis=1)
        for g in range(ATT_KV_HEADS):
            kc = k_ref[pl.ds(c * DSA_KB, DSA_KB), g].reshape(DSA_KC, HEAD_DIM)
            vc = v_ref[pl.ds(c * DSA_KB, DSA_KB), g].reshape(DSA_KC, HEAD_DIM)
            qg = q_ref[0, g * ATT_GROUP:(g + 1) * ATT_GROUP].reshape(ATT_GROUP * DSA_QB, HEAD_DIM)
            logit = _dot_nt(kc, qg) + bias4
            m_old = m_ref[g]
            m_new = jnp.maximum(m_old, jnp.max(logit, axis=0, keepdims=True))
            alpha = jnp.exp(m_old - m_new)
            p = jnp.exp(logit - m_new)
            s_ref[g] = alpha * s_ref[g] + jnp.sum(p, axis=0, keepdims=True)
            pv = lax.dot_general(vc, p.astype(BF16), (((0,), (0,)), ((), ())), preferred_element_type=F32)
            ot_ref[g] = ot_ref[g] * alpha + pv
            m_ref[g] = m_new
        return carry

    lax.fori_loop(0, n_chunks, attend_chunk, 0)

    for g in range(ATT_KV_HEADS):
        o_t = ot_ref[g] * (1.0 / s_ref[g])
        for hh in range(ATT_GROUP):
            head = g * ATT_GROUP + hh
            o_ref[:, head * HEAD_DIM:(head + 1) * HEAD_DIM] = o_t[:, hh * LANES:(hh + 1) * LANES].T.astype(o_ref.dtype)


def dsa_core(qk, v, iq, ik, iw, n_sel):
    nb = qk.shape[0]
    T = nb * LANES
    resident = lambda heads, first: pl.BlockSpec((nb, heads, LANES, LANES), lambda i: (0, first // heads, 0, 0),
                                                 pipeline_mode=pl.Buffered(1))
    return pl.pallas_call(
        functools.partial(_dsa_kernel, n_sel=n_sel),
        grid=(nb,),
        in_specs=[
            pl.BlockSpec((1, ATT_HEADS, LANES, LANES), lambda i: (i, 0, 0, 0)),
            pl.BlockSpec((1, IDX_HEADS, LANES, LANES), lambda i: (i, 0, 0, 0)),
            pl.BlockSpec((DSA_QB, LANES), lambda i: (i, 0)),
            resident(ATT_KV_HEADS, ATT_HEADS),
            resident(ATT_KV_HEADS, 0),
            resident(1, 0),
        ],
        out_specs=pl.BlockSpec((DSA_QB, ATT_HEADS * HEAD_DIM), lambda i: (i, 0)),
        out_shape=jax.ShapeDtypeStruct((T, ATT_HEADS * HEAD_DIM), BF16),
        scratch_shapes=[
            pltpu.VMEM((T, LANES), F32),
            pltpu.VMEM((ATT_KV_HEADS, HEAD_DIM, ATT_GROUP * DSA_QB), F32),
            pltpu.VMEM((ATT_KV_HEADS, 1, ATT_GROUP * DSA_QB), F32),
            pltpu.VMEM((ATT_KV_HEADS, 1, ATT_GROUP * DSA_QB), F32),
        ],
        compiler_params=_params("parallel"),
        name="dsa_core",
    )(qk, iq, iw, qk, v, ik)


def dsa_layer(h, gain, w_in, w_out):
    T, D = h.shape
    nq, nkv = ATT_HEADS * HEAD_DIM, ATT_KV_HEADS * HEAD_DIM
    ni = IDX_HEADS * HEAD_DIM
    w_qk = w_in[:, :nq + nkv].astype(BF16)
    w_v = w_in[:, nq + nkv:nq + 2 * nkv].astype(BF16)
    o_i = nq + 2 * nkv
    w_iq = w_in[:, o_i:o_i + ni].astype(BF16)
    w_ik = w_in[:, o_i + ni:o_i + ni + HEAD_DIM].astype(BF16)
    w_iw = jnp.zeros((D, LANES), F32).at[:, :IDX_HEADS].set(w_in[:, o_i + ni + HEAD_DIM:]).astype(BF16)
    qk = rope_proj(h, gain, w_qk, _rope_tables(T, HEAD_DIM), rot_dim=HEAD_DIM, scale=HEAD_DIM ** -0.5,
                   n_scaled=ATT_HEADS // 4)
    idx_tables = _rope_tables(T, IDX_ROPE_DIM)
    iq = rope_proj(h, gain, w_iq, idx_tables, rot_dim=IDX_ROPE_DIM)
    ik = rope_proj(h, gain, w_ik, idx_tables, rot_dim=IDX_ROPE_DIM)
    v = norm_proj(h, gain, w_v, out_dtype=BF16).reshape(T // LANES, LANES, ATT_KV_HEADS, HEAD_DIM).transpose(0, 2, 1, 3)
    iw = norm_proj(h, gain, w_iw) * (IDX_HEADS ** -0.5 * HEAD_DIM ** -0.5)
    o = dsa_core(qk, v, iq, ik, iw, min(TOPK_MAX, T // 4))
    return matmul_residual(o, w_out, h)


MOE_TILE = 256
MOE_EXPERT_VMEM_LIMIT_BYTES = 62 * 1024 * 1024


SLAB_LINES = D_MODEL // LANES
SLAB_PITCH = 20


def _store_slabs(ref, x):
    n = x.shape[0]
    for c in range(SLAB_LINES):
        ref[pl.ds(c, n, stride=SLAB_PITCH), :] = x[:, c * LANES:(c + 1) * LANES]
    for c in range(SLAB_LINES, SLAB_PITCH):
        ref[pl.ds(c, n, stride=SLAB_PITCH), :] = jnp.zeros((n, LANES), x.dtype)


def _load_slabs(ref, first_slab, n):
    return jnp.concatenate(
        [ref[pl.ds(first_slab * SLAB_PITCH + c, n, stride=SLAB_PITCH), :] for c in range(SLAB_LINES)], axis=1)


def _router_kernel(h_ref, g_ref, rw_ref, rb_ref, xn_ref, idx_ref, gate_ref):
    xn = _norm_rows(h_ref[...], g_ref[...])
    _store_slabs(xn_ref, xn)
    logits = jnp.dot(xn, rw_ref[...], preferred_element_type=F32, precision=lax.Precision.HIGHEST) + rb_ref[...]
    lane = lax.broadcasted_iota(I32, logits.shape, 1)
    vals, idxs = [], []
    for _ in range(MOE_TOP_K):
        m = jnp.max(logits, axis=-1, keepdims=True)
        sel = jnp.min(jnp.where(logits == m, lane, LANES), axis=-1, keepdims=True)
        vals.append(m)
        idxs.append(sel)
        logits = jnp.where(lane == sel, -jnp.inf, logits)
    ex = [jnp.exp(v - vals[0]) for v in vals]
    den = ex[0] + ex[1] + ex[2] + ex[3]
    idx_out = jnp.zeros(lane.shape, I32)
    gate_out = jnp.zeros(lane.shape, F32)
    for k in range(MOE_TOP_K):
        idx_out = jnp.where(lane == k, idxs[k], idx_out)
        gate_out = jnp.where(lane == k, ex[k] / den, gate_out)
    idx_ref[...] = idx_out
    gate_ref[...] = gate_out


def moe_router(h, gain, router_w, router_b, *, tm=512):
    T, D = h.shape
    rw = jnp.zeros((D, LANES), F32).at[:, :N_EXPERTS].set(router_w)
    rb = jnp.full((1, LANES), -jnp.inf, F32).at[0, :N_EXPERTS].set(router_b)
    return pl.pallas_call(
        _router_kernel,
        grid=(T // tm,),
        in_specs=[
            pl.BlockSpec((tm, D), lambda i: (i, 0)),
            pl.BlockSpec((1, D), lambda i: (0, 0)),
            pl.BlockSpec((D, LANES), lambda i: (0, 0)),
            pl.BlockSpec((1, LANES), lambda i: (0, 0)),
        ],
        out_specs=[
            pl.BlockSpec((tm * SLAB_PITCH, LANES), lambda i: (i, 0)),
            pl.BlockSpec((tm, LANES), lambda i: (i, 0)),
            pl.BlockSpec((tm, LANES), lambda i: (i, 0)),
        ],
        out_shape=[
            jax.ShapeDtypeStruct((T * SLAB_PITCH, LANES), F32),
            jax.ShapeDtypeStruct((T, LANES), I32),
            jax.ShapeDtypeStruct((T, LANES), F32),
        ],
        compiler_params=_params("parallel"),
        name="moe_router",
    )(h, gain.reshape(1, D), rw, rb)


def _route(idx, n_slots, tile):
    T = idx.shape[0]
    MOE_TILE = tile
    n_tiles = n_slots // MOE_TILE
    flat_e = idx.reshape(-1)
    onehot = (flat_e[:, None] == jnp.arange(N_EXPERTS, dtype=I32)[None, :]).astype(I32)
    csum = jnp.cumsum(onehot, axis=0)
    rank = jnp.sum(csum * onehot, axis=1) - 1
    counts = csum[-1]
    padded = ((counts + MOE_TILE - 1) // MOE_TILE) * MOE_TILE
    ends = jnp.cumsum(padded)
    starts = ends - padded
    slot = jnp.sum(starts[None, :] * onehot, axis=1) + rank
    token = jnp.arange(T * MOE_TOP_K, dtype=I32) // MOE_TOP_K
    tok_of_slot = jnp.zeros((n_slots,), I32).at[slot].set(token, unique_indices=True)
    tile_start = jnp.arange(n_tiles, dtype=I32) * MOE_TILE
    tile_expert = jnp.sum((tile_start[:, None] >= ends[None, :]).astype(I32), axis=1)
    tile_expert = jnp.minimum(tile_expert, N_EXPERTS - 1)
    n_used = ends[-1] // MOE_TILE
    return tok_of_slot, slot.reshape(T, MOE_TOP_K), tile_expert, n_used.reshape(1)


def _start_slab_gather(ids_ref, n_rows, src_hbm, dst_ref, sem, *, both_queues=False):
    for r in range(n_rows):
        line = ids_ref[0, 0, r]
        copy = pltpu.make_async_copy(src_hbm.at[pl.ds(line, SLAB_LINES), :],
                                     dst_ref.at[pl.ds(r * SLAB_PITCH, SLAB_LINES), :], sem)
        copy.start(priority=r % 2 if both_queues else 0)


def _wait_slab_gather(n_rows, src_hbm, dst_ref, sem):
    n = n_rows * SLAB_LINES
    pltpu.make_async_copy(src_hbm.at[pl.ds(0, n), :], dst_ref.at[pl.ds(0, n), :], sem).wait()


def _expert_kernel(texp_ref, nused_ref, ids_cur_ref, ids_nxt_ref, x_hbm, w1_ref, b1_ref, w2_ref, b2_ref,
                   y_ref, xbuf, sem, *, tile):
    t = pl.program_id(0)
    n_used = nused_ref[0]
    slot = t % 2

    @pl.when(jnp.logical_and(t == 0, n_used > 0))
    def _():
        _start_slab_gather(ids_cur_ref, tile, x_hbm, xbuf.at[0], sem.at[0])

    @pl.when(t + 1 < n_used)
    def _():
        _start_slab_gather(ids_nxt_ref, tile, x_hbm, xbuf.at[1 - slot], sem.at[1 - slot])

    @pl.when(t < n_used)
    def _():
        _wait_slab_gather(tile, x_hbm, xbuf.at[slot], sem.at[slot])
        x = _load_slabs(xbuf.at[slot], 0, tile).astype(BF16)
        hcat = jnp.dot(x, w1_ref[0].astype(BF16), preferred_element_type=F32) + b1_ref[0]
        gate = jnp.minimum(hcat[:, :D_EXPERT], SWIGLU_LIMIT)
        up = jnp.clip(hcat[:, D_EXPERT:], -SWIGLU_LIMIT, SWIGLU_LIMIT)
        glu = gate * _sigmoid(SWIGLU_ALPHA * gate)
        act = ((up + 1.0) * glu).astype(BF16)
        _store_slabs(y_ref, jnp.dot(act, w2_ref[0].astype(BF16), preferred_element_type=F32) + b2_ref[0])

    @pl.when(t >= n_used)
    def _():
        y_ref[...] = jnp.zeros(y_ref.shape, F32)


def moe_experts(xn_slabs, tok_of_slot, tile_expert, n_used, w1, b1, w2, b2, tile):
    D = D_MODEL
    n_slots = tok_of_slot.shape[0]
    n_tiles = n_slots // tile
    ids = (tok_of_slot * SLAB_PITCH).reshape(n_tiles, 1, tile)
    grid_spec = pltpu.PrefetchScalarGridSpec(
        num_scalar_prefetch=2,
        grid=(n_tiles,),
        in_specs=[
            pl.BlockSpec((1, 1, tile), lambda t, te, nu: (t, 0, 0), memory_space=pltpu.SMEM),
            pl.BlockSpec((1, 1, tile), lambda t, te, nu: (jnp.minimum(t + 1, n_tiles - 1), 0, 0),
                         memory_space=pltpu.SMEM),
            pl.BlockSpec(memory_space=pl.ANY),
            pl.BlockSpec((1, D, 2 * D_EXPERT), lambda t, te, nu: (te[t], 0, 0)),
            pl.BlockSpec((1, 1, 2 * D_EXPERT), lambda t, te, nu: (te[t], 0, 0)),
            pl.BlockSpec((1, D_EXPERT, D), lambda t, te, nu: (te[t], 0, 0)),
            pl.BlockSpec((1, 1, D), lambda t, te, nu: (te[t], 0, 0)),
        ],
        out_specs=pl.BlockSpec((tile * SLAB_PITCH, LANES), lambda t, te, nu: (t, 0)),
        scratch_shapes=[pltpu.VMEM((2, tile * SLAB_PITCH, LANES), F32), pltpu.SemaphoreType.DMA((2,))],
    )
    return pl.pallas_call(
        functools.partial(_expert_kernel, tile=tile),
        grid_spec=grid_spec,
        out_shape=jax.ShapeDtypeStruct((n_slots * SLAB_PITCH, LANES), F32),
        compiler_params=pltpu.CompilerParams(dimension_semantics=("arbitrary",),
                                             vmem_limit_bytes=MOE_EXPERT_VMEM_LIMIT_BYTES),
        name="moe_experts",
    )(tile_expert, n_used, ids, ids, xn_slabs, w1, b1.reshape(N_EXPERTS, 1, -1), w2, b2.reshape(N_EXPERTS, 1, -1))


COMBINE_TOKENS = 128


def _combine_kernel(ids_cur_ref, ids_nxt_ref, y_hbm, gate_ref, h_ref, *rest):
    out_gain_ref = rest[0] if len(rest) == 4 else None
    o_ref, ybuf, sem = rest[-3:]
    t = pl.program_id(0)
    n = pl.num_programs(0)
    slot = t % 2
    rows = COMBINE_TOKENS * MOE_TOP_K

    @pl.when(t == 0)
    def _():
        _start_slab_gather(ids_cur_ref, rows, y_hbm, ybuf.at[0], sem.at[0], both_queues=True)

    @pl.when(t + 1 < n)
    def _():
        _start_slab_gather(ids_nxt_ref, rows, y_hbm, ybuf.at[1 - slot], sem.at[1 - slot], both_queues=True)

    _wait_slab_gather(rows, y_hbm, ybuf.at[slot], sem.at[slot])
    gates = gate_ref[...]
    acc = h_ref[...]
    for k in range(MOE_TOP_K):
        acc = acc + gates[:, k:k + 1] * _load_slabs(ybuf.at[slot], k * COMBINE_TOKENS, COMBINE_TOKENS)
    o_ref[...] = acc if out_gain_ref is None else _norm_rows(acc, out_gain_ref[...])


def moe_combine(y_slabs, slot_of, gate_pad, h, out_gain=None):
    T, D = h.shape
    gain_args = [] if out_gain is None else [out_gain.reshape(1, D)]
    gain_specs = [] if out_gain is None else [pl.BlockSpec((1, D), lambda t: (0, 0))]
    n_steps = T // COMBINE_TOKENS
    rows = COMBINE_TOKENS * MOE_TOP_K
    ids = (slot_of * SLAB_PITCH).reshape(n_steps, COMBINE_TOKENS, MOE_TOP_K).transpose(0, 2, 1)
    ids = ids.reshape(n_steps, 1, rows)
    return pl.pallas_call(
        _combine_kernel,
        grid=(n_steps,),
        in_specs=[
            pl.BlockSpec((1, 1, rows), lambda t: (t, 0, 0), memory_space=pltpu.SMEM),
            pl.BlockSpec((1, 1, rows), lambda t: (jnp.minimum(t + 1, n_steps - 1), 0, 0), memory_space=pltpu.SMEM),
            pl.BlockSpec(memory_space=pl.ANY),
            pl.BlockSpec((COMBINE_TOKENS, LANES), lambda t: (t, 0)),
            pl.BlockSpec((COMBINE_TOKENS, D), lambda t: (t, 0)),
        ] + gain_specs,
        out_specs=pl.BlockSpec((COMBINE_TOKENS, D), lambda t: (t, 0)),
        out_shape=jax.ShapeDtypeStruct((T, D), F32),
        scratch_shapes=[pltpu.VMEM((2, rows * SLAB_PITCH, LANES), F32), pltpu.SemaphoreType.DMA((2,))],
        compiler_params=_params("arbitrary"),
        name="moe_combine",
    )(ids, ids, y_slabs, gate_pad, h, *gain_args)


def moe_layer(h, gain, router_w, router_b, w1, b1, w2, b2, out_gain=None):
    T = h.shape[0]
    n_slots = T * MOE_TOP_K + N_EXPERTS * MOE_TILE
    xn_slabs, idx_pad, gate_pad = moe_router(h, gain, router_w, router_b)
    tok_of_slot, slot_of, tile_expert, n_used = _route(idx_pad[:, :MOE_TOP_K], n_slots, MOE_TILE)
    y_slabs = moe_experts(xn_slabs, tok_of_slot, tile_expert, n_used, w1, b1, w2, b2, MOE_TILE)
    return moe_combine(y_slabs, slot_of, gate_pad, h, out_gain)


def kernel(x, hgrn_lb_logits,
           l0_norm_mix, l0_conv_in, l0_conv_w, l0_conv_out, l0_norm_ffn,
           l0_router_w, l0_router_b, l0_exp_w1, l0_exp_b1, l0_exp_w2, l0_exp_b2,
           l1_norm_mix, l1_hgrn_in, l1_hgrn_gnorm, l1_hgrn_out, l1_norm_ffn,
           l1_router_w, l1_router_b, l1_exp_w1, l1_exp_b1, l1_exp_w2, l1_exp_b2,
           l2_norm_mix, l2_dsa_in, l2_dsa_out, l2_norm_ffn,
           l2_router_w, l2_router_b, l2_exp_w1, l2_exp_b1, l2_exp_w2, l2_exp_b2,
           l3_norm_mix, l3_conv_in, l3_conv_w, l3_conv_out, l3_norm_ffn,
           l3_router_w, l3_router_b, l3_exp_w1, l3_exp_b1, l3_exp_w2, l3_exp_b2,
           final_norm_gain):
    B, T, D = x.shape
    bf = lambda w: w.astype(BF16)
    h = x.reshape(B * T, D)

    h = short_conv_layer(h, l0_norm_mix, bf(l0_conv_in), l0_conv_w, bf(l0_conv_out))
    h = moe_layer(h, l0_norm_ffn, l0_router_w, l0_router_b, l0_exp_w1, l0_exp_b1, l0_exp_w2, l0_exp_b2)

    lb_soft = jax.nn.softmax(hgrn_lb_logits.astype(F32), axis=0)
    lb_all = jnp.cumsum(lb_soft, axis=0) - lb_soft[0:1]
    h = hgrn2_layer(h, l1_norm_mix, bf(l1_hgrn_in), l1_hgrn_gnorm, bf(l1_hgrn_out), lb_all[1])
    h = moe_layer(h, l1_norm_ffn, l1_router_w, l1_router_b, l1_exp_w1, l1_exp_b1, l1_exp_w2, l1_exp_b2)

    h = dsa_layer(h, l2_norm_mix, l2_dsa_in, bf(l2_dsa_out))
    h = moe_layer(h, l2_norm_ffn, l2_router_w, l2_router_b, l2_exp_w1, l2_exp_b1, l2_exp_w2, l2_exp_b2)

    h = short_conv_layer(h, l3_norm_mix, bf(l3_conv_in), l3_conv_w, bf(l3_conv_out))
    out = moe_layer(h, l3_norm_ffn, l3_router_w, l3_router_b, l3_exp_w1, l3_exp_b1, l3_exp_w2, l3_exp_b2,
                    out_gain=final_norm_gain)
    return out.reshape(B, T, D)
```
